```python
import math
import jax, jax.numpy as jnp
from jax import lax
import numpy as np

D_MODEL = 1024
BATCH = 8
SEQ = 4096
DEPTH = 2

CHUNK = 64
PLE_DIM = 256
D_POOL = D_MODEL // 2
D_CONV = D_MODEL - D_POOL
POOL_WINDOWS = (2, 4, 8, 16)
POOL_GROUP = D_POOL // len(POOL_WINDOWS)
CONV_KERNEL = 31
HEAD_DIM = 64
N_HEADS = D_MODEL // HEAD_DIM
LEFT_CHUNKS = 8
BAND = (LEFT_CHUNKS + 1) * CHUNK
MAX_REL_DIST = 256
D_FF = ((8 * D_MODEL // 3) + 127) // 128 * 128
FFN_CONV_KERNEL = 3
N_EVEN = (DEPTH + 1) // 2
N_ODD = DEPTH // 2
DEEPNORM_ALPHA = (2 * DEPTH) ** 0.25
DEEPNORM_BETA = (8 * DEPTH) ** -0.25
LN_EPS = 1e-5
NEG_INF = -1e30

kernel_name = "hybrid_pool_conv_chunkattn_encoder"


def layer_norm(x, g, b):
    x32 = x.astype(jnp.float32)
    mu = jnp.mean(x32, axis=-1, keepdims=True)
    var = jnp.mean(jnp.square(x32 - mu), axis=-1, keepdims=True)
    y = (x32 - mu) * lax.rsqrt(var + LN_EPS)
    return (y * g.astype(jnp.float32) + b.astype(jnp.float32)).astype(x.dtype)


def causal_dwconv(x, w, b):
    k = w.shape[0]
    c = x.shape[-1]
    y = lax.conv_general_dilated(
        x, w[:, None, :].astype(x.dtype), window_strides=(1,), padding=[(k - 1, 0)],
        dimension_numbers=('NWC', 'WIO', 'NWC'), feature_group_count=c)
    return y + b


def pool_conv_mixer(x, w_in, pool_w, pool_scale, dw_w, dw_b, cn_g, cn_b, w_out):
    bsz, s, _ = x.shape
    u = x @ w_in
    a = u[..., :D_POOL]
    b_val = u[..., D_POOL:D_POOL + D_CONV]
    b_gate = u[..., D_POOL + D_CONV:]

    a32 = a.astype(jnp.float32)
    cs = jnp.pad(jnp.cumsum(a32, axis=1), ((0, 0), (1, 0), (0, 0)))
    pos = jnp.arange(1, s + 1, dtype=jnp.float32)
    groups = []
    for g, w in enumerate(POOL_WINDOWS):
        sl = slice(g * POOL_GROUP, (g + 1) * POOL_GROUP)
        csg = cs[..., sl]
        lower = jnp.pad(csg[:, :s + 1 - w], ((0, 0), (w - 1, 0), (0, 0)))
        mean = (csg[:, 1:] - lower) / jnp.minimum(pos, float(w))[None, :, None]
        groups.append(mean - a32[..., sl])
    d = jnp.stack(groups, axis=2).astype(x.dtype)
    y_a = jnp.einsum('bsgc,gcd->bsgd', d, pool_w).reshape(bsz, s, D_POOL) * pool_scale

    glu = b_val * jax.nn.sigmoid(b_gate)
    h = causal_dwconv(glu, dw_w, dw_b)
    y_b = jax.nn.silu(layer_norm(h, cn_g, cn_b))

    return jnp.concatenate([y_a, y_b], axis=-1) @ w_out


def chunked_rel_attention(x, w_qkv, rel_bias, w_o):
    bsz, s, _ = x.shape
    nc = s // CHUNK
    pad = LEFT_CHUNKS * CHUNK
    q, k, v = jnp.split(x @ w_qkv, 3, axis=-1)
    q = q.reshape(bsz, nc, CHUNK, N_HEADS, HEAD_DIM).transpose(1, 0, 2, 3, 4)
    k = jnp.pad(k.reshape(bsz, s, N_HEADS, HEAD_DIM), ((0, 0), (pad, 0), (0, 0), (0, 0)))
    v = jnp.pad(v.reshape(bsz, s, N_HEADS, HEAD_DIM), ((0, 0), (pad, 0), (0, 0), (0, 0)))

    qi = jnp.arange(CHUNK)[:, None]
    kj = jnp.arange(BAND)[None, :]
    rel = jnp.clip(pad + qi - kj, -MAX_REL_DIST, MAX_REL_DIST) + MAX_REL_DIST
    bias = rel_bias[:, rel].astype(jnp.float32)
    scale = HEAD_DIM ** -0.5

    def one_chunk(args):
        qc, c = args
        kb = lax.dynamic_slice_in_dim(k, c * CHUNK, BAND, axis=1)
        vb = lax.dynamic_slice_in_dim(v, c * CHUNK, BAND, axis=1)
        sc = jnp.einsum('bqhd,bkhd->bhqk', qc, kb).astype(jnp.float32) * scale + bias
        key_pos = c * CHUNK - pad + jnp.arange(BAND)
        sc = jnp.where((key_pos >= 0)[None, None, None, :], sc, NEG_INF)
        pr = jax.nn.softmax(sc, axis=-1).astype(vb.dtype)
        return jnp.einsum('bhqk,bkhd->bqhd', pr, vb)

    out = lax.map(one_chunk, (q, jnp.arange(nc)))
    out = out.transpose(1, 0, 2, 3, 4).reshape(bsz, s, D_MODEL)
    return out @ w_o


def conv_ffn(x, w_up, dw_w, dw_b, w_down):
    gate, val = jnp.split(x @ w_up, 2, axis=-1)
    gate = causal_dwconv(gate, dw_w, dw_b)
    return (jax.nn.gelu(gate) * val) @ w_down


def setup_inputs(seed: int = 0) -> dict:
    key = jax.random.key(seed)
    ks = jax.random.split(key, 32)
    f32 = jnp.float32

    def nrm(k, shape, scale):
        return jax.random.normal(k, shape, f32) * scale

    d_in_even = D_POOL + 2 * D_CONV
    return {
        "x": nrm(ks[0], (BATCH, SEQ, D_MODEL), 1.0),
        "p": nrm(ks[1], (DEPTH, BATCH, SEQ, PLE_DIM), 1.0),
        "mix_w_in": nrm(ks[2], (N_EVEN, D_MODEL, d_in_even), D_MODEL ** -0.5),
        "pool_w": nrm(ks[3], (N_EVEN, len(POOL_WINDOWS), POOL_GROUP, POOL_GROUP), POOL_GROUP ** -0.5),
        "pool_scale": 1.0 + nrm(ks[4], (N_EVEN, D_POOL), 0.1),
        "conv_dw_w": nrm(ks[5], (N_EVEN, CONV_KERNEL, D_CONV), CONV_KERNEL ** -0.5),
        "conv_dw_b": nrm(ks[6], (N_EVEN, D_CONV), 0.02),
        "conv_ln_g": 1.0 + nrm(ks[7], (N_EVEN, D_CONV), 0.02),
        "conv_ln_b": nrm(ks[8], (N_EVEN, D_CONV), 0.02),
        "mix_w_out": nrm(ks[9], (N_EVEN, D_MODEL, D_MODEL), D_MODEL ** -0.5 * DEEPNORM_BETA),
        "attn_w_qkv": nrm(ks[10], (N_ODD, D_MODEL, 3 * D_MODEL), D_MODEL ** -0.5),
        "attn_rel_bias": nrm(ks[11], (N_ODD, N_HEADS, 2 * MAX_REL_DIST + 1), 0.5),
        "attn_w_o": nrm(ks[12], (N_ODD, D_MODEL, D_MODEL), D_MODEL ** -0.5 * DEEPNORM_BETA),
        "ln_mix_g": 1.0 + nrm(ks[13], (DEPTH, D_MODEL), 0.02),
        "ln_mix_b": nrm(ks[14], (DEPTH, D_MODEL), 0.02),
        "ffn_w_up": nrm(ks[15], (DEPTH, D_MODEL, 2 * D_FF), D_MODEL ** -0.5),
        "ffn_dw_w": nrm(ks[16], (DEPTH, FFN_CONV_KERNEL, D_FF), FFN_CONV_KERNEL ** -0.5),
        "ffn_dw_b": nrm(ks[17], (DEPTH, D_FF), 0.02),
        "ffn_w_down": nrm(ks[18], (DEPTH, D_FF, D_MODEL), D_FF ** -0.5 * DEEPNORM_BETA),
        "ple_w_proj": nrm(ks[19], (DEPTH, PLE_DIM, D_MODEL), PLE_DIM ** -0.5),
        "ple_w_gate": nrm(ks[20], (DEPTH, D_MODEL, D_MODEL), D_MODEL ** -0.5),
        "ple_b_gate": nrm(ks[21], (DEPTH, D_MODEL), 0.02),
        "ln_ffn_g": 1.0 + nrm(ks[22], (DEPTH, D_MODEL), 0.02),
        "ln_ffn_b": nrm(ks[23], (DEPTH, D_MODEL), 0.02),
    }


def reference(x, p, mix_w_in, pool_w, pool_scale, conv_dw_w, conv_dw_b, conv_ln_g,
              conv_ln_b, mix_w_out, attn_w_qkv, attn_rel_bias, attn_w_o, ln_mix_g,
              ln_mix_b, ffn_w_up, ffn_dw_w, ffn_dw_b, ffn_w_down, ple_w_proj,
              ple_w_gate, ple_b_gate, ln_ffn_g, ln_ffn_b):
    for i in range(DEPTH):
        j = i // 2
        if i % 2 == 0:
            mix = pool_conv_mixer(x, mix_w_in[j], pool_w[j], pool_scale[j], conv_dw_w[j],
                                  conv_dw_b[j], conv_ln_g[j], conv_ln_b[j], mix_w_out[j])
        else:
            mix = chunked_rel_attention(x, attn_w_qkv[j], attn_rel_bias[j], attn_w_o[j])
        x = layer_norm(DEEPNORM_ALPHA * x + mix, ln_mix_g[i], ln_mix_b[i])
        ffn = conv_ffn(x, ffn_w_up[i], ffn_dw_w[i], ffn_dw_b[i], ffn_w_down[i])
        gate = jax.nn.sigmoid(x @ ple_w_gate[i] + ple_b_gate[i])
        ple = gate * (p[i] @ ple_w_proj[i])
        x = layer_norm(DEEPNORM_ALPHA * x + ffn + ple, ln_ffn_g[i], ln_ffn_b[i])
    return x
```

```python
import functools

import jax
import jax.numpy as jnp
from jax import lax
from jax.experimental import pallas as pl
from jax.experimental.pallas import tpu as pltpu

F32 = jnp.float32
BF16 = jnp.bfloat16

D_MODEL = 1024
DEPTH = 2
CHUNK = 64
PLE_DIM = 256
D_POOL = 512
D_CONV = 512
POOL_WINDOWS = (2, 4, 8, 16)
POOL_GROUP = 128
CONV_KERNEL = 31
HEAD_DIM = 64
N_HEADS = 16
LEFT_CHUNKS = 8
MAX_REL_DIST = 256
D_FF = 2816
FFN_CONV_KERNEL = 3
DEEPNORM_ALPHA = (2 * DEPTH) ** 0.25
LN_EPS = 1e-5
NEG_INF = -1e30

SUBLANES = 8
LANES = 128
VMEM_LIMIT_BYTES = 56 * 1024 * 1024

MIX_TM = 512
MIX_RC = 32
POOL_HALO = 16
CONV_HALO = 32
FFN_TM = 512
FFN_TF = 1408
FFN_HALO = SUBLANES
QKV_TM = 512
ATT_TQ = 256
ATT_HIST = LEFT_CHUNKS * CHUNK
ATT_NK = ATT_HIST + ATT_TQ
HEADS_PER_STEP = LANES // HEAD_DIM
OPROJ_TM = 512


def _layer_norm(v, g, b):
    mu = jnp.mean(v, axis=-1, keepdims=True)
    c = v - mu
    var = jnp.mean(c * c, axis=-1, keepdims=True)
    return c * lax.rsqrt(var + LN_EPS) * g + b


def _params(semantics):
    return pltpu.CompilerParams(dimension_semantics=semantics,
                                vmem_limit_bytes=VMEM_LIMIT_BYTES)


def _const_spec(shape):
    zeros = (0,) * len(shape)
    return pl.BlockSpec(shape, lambda *_: zeros)


def _mixer_kernel(x_ref, w_in_ref, pool_w_ref, pool_scale_ref, dw_w_ref, dw_b_ref,
                  cn_g_ref, cn_b_ref, w_out_ref, ln_g_ref, ln_b_ref, o_ref,
                  a_ext, g_ext, g_ph, cat):
    i = pl.program_id(1)
    tm = x_ref.shape[0]

    @pl.when(i == 0)
    def _():
        a_ext[0:POOL_HALO, :] = jnp.zeros((POOL_HALO, D_POOL), F32)
        g_ext[0:CONV_HALO, :] = jnp.zeros((CONV_HALO, D_CONV), F32)

    @pl.when(i > 0)
    def _():
        a_ext[0:POOL_HALO, :] = a_ext[tm:tm + POOL_HALO, :]
        g_ext[0:CONV_HALO, :] = g_ext[tm:tm + CONV_HALO, :]

    xb = x_ref[...]
    u = jnp.dot(xb.astype(BF16), w_in_ref[...], preferred_element_type=F32)
    a_ext[POOL_HALO:, :] = u[:, :D_POOL]
    g_ext[CONV_HALO:, :] = (u[:, D_POOL:D_POOL + D_CONV]
                            * jax.nn.sigmoid(u[:, D_POOL + D_CONV:]))

    row = lax.broadcasted_iota(jnp.int32, (tm, POOL_GROUP), 0)
    pos = (row + (i * tm + 1)).astype(F32)
    for g, w in enumerate(POOL_WINDOWS):
        cols = slice(g * POOL_GROUP, (g + 1) * POOL_GROUP)
        cur = a_ext[POOL_HALO:POOL_HALO + tm, cols]
        s = cur
        for j in range(1, w):
            s = s + a_ext[POOL_HALO - j:POOL_HALO - j + tm, cols]
        d = s / jnp.minimum(pos, float(w)) - cur
        ya = jnp.dot(d.astype(BF16), pool_w_ref[g], preferred_element_type=F32)
        cat[:, cols] = (ya * pool_scale_ref[:, cols]).astype(BF16)

    for p in range(1, SUBLANES):
        g_ph[p - 1, :, :] = g_ext[p:p + g_ph.shape[1], :]

    def conv_chunk(c, carry):
        r0 = pl.multiple_of(c * MIX_RC, MIX_RC)
        acc = None
        for k in range(CONV_KERNEL):
            off = CONV_HALO - (CONV_KERNEL - 1) + k
            q, p = divmod(off, SUBLANES)
            if p == 0:
                t = g_ext[pl.ds(r0 + SUBLANES * q, MIX_RC), :]
            else:
                t = g_ph[p - 1, pl.ds(r0 + SUBLANES * q, MIX_RC), :]
            t = t * dw_w_ref[k:k + 1, :]
            acc = t if acc is None else acc + t
        h = _layer_norm(acc + dw_b_ref[...], cn_g_ref[...], cn_b_ref[...])
        cat[pl.ds(r0, MIX_RC), D_POOL:] = (h * jax.nn.sigmoid(h)).astype(BF16)
        return carry

    lax.fori_loop(0, tm // MIX_RC, conv_chunk, 0)

    mix = jnp.dot(cat[...], w_out_ref[...], preferred_element_type=F32)
    o_ref[...] = _layer_norm(DEEPNORM_ALPHA * xb + mix, ln_g_ref[...], ln_b_ref[...])


def _mixer(x, w_in, pool_w, pool_scale, dw_w, dw_b, cn_g, cn_b, w_out, ln_g, ln_b):
    bsz, s, d = x.shape
    tm = MIX_TM
    d_in = w_in.shape[1]
    return pl.pallas_call(
        _mixer_kernel,
        grid=(bsz, s // tm),
        in_specs=[
            pl.BlockSpec((None, tm, d), lambda b, i: (b, i, 0)),
            _const_spec((d, d_in)),
            _const_spec(pool_w.shape),
            _const_spec((1, D_POOL)),
            _const_spec((CONV_KERNEL, D_CONV)),
            _const_spec((1, D_CONV)),
            _const_spec((1, D_CONV)),
            _const_spec((1, D_CONV)),
            _const_spec((d, d)),
            _const_spec((1, d)),
            _const_spec((1, d)),
        ],
        out_specs=pl.BlockSpec((None, tm, d), lambda b, i: (b, i, 0)),
        out_shape=jax.ShapeDtypeStruct(x.shape, F32),
        scratch_shapes=[
            pltpu.VMEM((tm + POOL_HALO, D_POOL), F32),
            pltpu.VMEM((tm + CONV_HALO, D_CONV), F32),
            pltpu.VMEM((SUBLANES - 1, tm + CONV_HALO - SUBLANES, D_CONV), F32),
            pltpu.VMEM((tm, d), BF16),
        ],
        compiler_params=_params(("arbitrary", "arbitrary")),
        name="mixer0",
    )(x, w_in, pool_w, pool_scale, dw_w, dw_b, cn_g, cn_b, w_out, ln_g, ln_b)


def _ffn_kernel(x_ref, p_ref, wg_ref, wv_ref, dw_w_ref, dw_b_ref, wd_ref,
                wproj_ref, wpg_ref, bpg_ref, ln_g_ref, ln_b_ref, o_ref,
                acc, carry, g_ext):
    i = pl.program_id(1)
    j = pl.program_id(2)
    nj = pl.num_programs(2)
    tm = x_ref.shape[0]

    xb = x_ref[...]
    xb16 = xb.astype(BF16)
    gate = jnp.dot(xb16, wg_ref[...], preferred_element_type=F32)
    val = jnp.dot(xb16, wv_ref[...], preferred_element_type=F32)

    @pl.when(i == 0)
    def _():
        g_ext[0:FFN_HALO, :] = jnp.zeros((FFN_HALO, g_ext.shape[1]), F32)

    @pl.when(i > 0)
    def _():
        g_ext[0:FFN_HALO, :] = carry[j]

    g_ext[FFN_HALO:, :] = gate
    carry[j] = gate[tm - FFN_HALO:, :]

    conv = dw_b_ref[...] + gate * dw_w_ref[FFN_CONV_KERNEL - 1:FFN_CONV_KERNEL, :]
    for k in range(FFN_CONV_KERNEL - 1):
        back = FFN_CONV_KERNEL - 1 - k
        conv = conv + g_ext[FFN_HALO - back:FFN_HALO - back + tm, :] * dw_w_ref[k:k + 1, :]
    act = (jax.nn.gelu(conv) * val).astype(BF16)
    part = jnp.dot(act, wd_ref[...], preferred_element_type=F32)

    @pl.when(j == 0)
    def _():
        acc[...] = part

    @pl.when(j > 0)
    def _():
        acc[...] += part

    @pl.when(j == nj - 1)
    def _():
        gate_p = jax.nn.sigmoid(
            jnp.dot(xb16, wpg_ref[...], preferred_element_type=F32) + bpg_ref[...])
        ple = gate_p * jnp.dot(p_ref[...].astype(BF16), wproj_ref[...],
                               preferred_element_type=F32)
        y = DEEPNORM_ALPHA * xb + acc[...] + ple
        o_ref[...] = _layer_norm(y, ln_g_ref[...], ln_b_ref[...])


def _ffn(x, p, layer, w_up, dw_w, dw_b, w_down, w_proj, w_pgate, b_pgate, ln_g, ln_b):
    bsz, s, d = x.shape
    tm, tf = FFN_TM, FFN_TF
    nj = D_FF // tf
    return pl.pallas_call(
        _ffn_kernel,
        grid=(bsz, s // tm, nj),
        in_specs=[
            pl.BlockSpec((None, tm, d), lambda b, i, j: (b, i, 0)),
            pl.BlockSpec((None, None, tm, PLE_DIM), lambda b, i, j: (layer, b, i, 0)),
            pl.BlockSpec((d, tf), lambda b, i, j: (0, j)),
            pl.BlockSpec((d, tf), lambda b, i, j: (0, j + nj)),
            pl.BlockSpec((FFN_CONV_KERNEL, tf), lambda b, i, j: (0, j)),
            pl.BlockSpec((1, tf), lambda b, i, j: (0, j)),
            pl.BlockSpec((tf, d), lambda b, i, j: (j, 0)),
            _const_spec((PLE_DIM, d)),
            _const_spec((d, d)),
            _const_spec((1, d)),
            _const_spec((1, d)),
            _const_spec((1, d)),
        ],
        out_specs=pl.BlockSpec((None, tm, d), lambda b, i, j: (b, i, 0)),
        out_shape=jax.ShapeDtypeStruct(x.shape, F32),
        scratch_shapes=[
            pltpu.VMEM((tm, d), F32),
            pltpu.VMEM((nj, FFN_HALO, tf), F32),
            pltpu.VMEM((tm + FFN_HALO, tf), F32),
        ],
        compiler_params=_params(("arbitrary", "arbitrary", "arbitrary")),
        name="convffn",
    )(x, p, w_up, w_up, dw_w, dw_b, w_down, w_proj, w_pgate, b_pgate, ln_g, ln_b)


_NT = (((1,), (1,)), ((), ()))
_TN = (((0,), (0,)), ((), ()))


def _qkv_kernel(x_ref, wqt_ref, wk_ref, wvt_ref, qt_ref, k_ref, vt_ref):
    xb16 = x_ref[...].astype(BF16)
    qt_ref[...] = lax.dot_general(wqt_ref[...], xb16, _NT,
                                  preferred_element_type=F32).astype(BF16)
    k_ref[...] = jnp.dot(xb16, wk_ref[...], preferred_element_type=F32).astype(BF16)
    vt_ref[...] = lax.dot_general(wvt_ref[...], xb16, _NT,
                                  preferred_element_type=F32).astype(BF16)


def _qkv(x, wqt, wk, wvt):
    bsz, s, d = x.shape
    tm = QKV_TM
    row_spec = pl.BlockSpec((None, tm, d), lambda b, i: (b, i, 0))
    col_spec = pl.BlockSpec((None, d, tm), lambda b, i: (b, 0, i))
    return pl.pallas_call(
        _qkv_kernel,
        grid=(bsz, s // tm),
        in_specs=[row_spec, _const_spec((d, d)), _const_spec((d, d)), _const_spec((d, d))],
        out_specs=[col_spec, row_spec, col_spec],
        out_shape=[jax.ShapeDtypeStruct((bsz, d, s), BF16),
                   jax.ShapeDtypeStruct((bsz, s, d), BF16),
                   jax.ShapeDtypeStruct((bsz, d, s), BF16)],
        compiler_params=_params(("arbitrary", "arbitrary")),
        name="qkv",
    )(x, wqt, wk, wvt)


def _attn_kernel(qt_ref, k_ref, vt_ref, bias_ref, ot_ref):
    s_len = k_ref.shape[0]
    tq = ATT_TQ
    head_of_row = lax.broadcasted_iota(jnp.int32, (LANES, tq), 0) // HEAD_DIM

    def block(q0, nk):
        k0 = q0 + tq - nk
        if not isinstance(k0, int):
            k0 = pl.multiple_of(k0, tq)
        kwin = k_ref[pl.ds(k0, nk), :]
        qblk = qt_ref[:, pl.ds(q0, tq)]
        for h in range(HEADS_PER_STEP):
            qh = jnp.where(head_of_row == h, qblk, jnp.zeros_like(qblk))
            sc = jnp.dot(kwin, qh, preferred_element_type=F32)
            sc = sc + bias_ref[h, ATT_NK - nk:, :]
            m = jnp.max(sc, axis=0, keepdims=True)
            pr = jnp.exp(sc - m)
            denom = jnp.sum(pr, axis=0, keepdims=True)
            vwin = vt_ref[h * HEAD_DIM:(h + 1) * HEAD_DIM, pl.ds(k0, nk)]
            o = jnp.dot(vwin, pr.astype(BF16), preferred_element_type=F32)
            ot_ref[h * HEAD_DIM:(h + 1) * HEAD_DIM, pl.ds(q0, tq)] = (o / denom).astype(BF16)

    n_short = ATT_HIST // tq
    for b in range(n_short):
        block(b * tq, (b + 1) * tq)

    def body(b, carry):
        block(pl.multiple_of(b * tq, tq), ATT_NK)
        return carry

    lax.fori_loop(n_short, s_len // tq, body, 0)


def _attention(qt, k, vt, bias_t):
    bsz, s, d = k.shape
    n_groups = d // LANES
    lane_spec = pl.BlockSpec((None, LANES, s), lambda g, b: (b, g, 0))
    return pl.pallas_call(
        _attn_kernel,
        grid=(n_groups, bsz),
        in_specs=[
            lane_spec,
            pl.BlockSpec((None, s, LANES), lambda g, b: (b, 0, g)),
            lane_spec,
            pl.BlockSpec((HEADS_PER_STEP, ATT_NK, ATT_TQ), lambda g, b: (g, 0, 0)),
        ],
        out_specs=lane_spec,
        out_shape=jax.ShapeDtypeStruct((bsz, d, s), BF16),
        compiler_params=_params(("arbitrary", "arbitrary")),
        name="band_attn",
    )(qt, k, vt, bias_t)


def _oproj_kernel(ot_ref, x_ref, wo_ref, ln_g_ref, ln_b_ref, o_ref):
    mix = lax.dot_general(ot_ref[...], wo_ref[...], _TN, preferred_element_type=F32)
    o_ref[...] = _layer_norm(DEEPNORM_ALPHA * x_ref[...] + mix, ln_g_ref[...], ln_b_ref[...])


def _oproj(ot, x, wo, ln_g, ln_b):
    bsz, s, d = x.shape
    tm = OPROJ_TM
    row_spec = pl.BlockSpec((None, tm, d), lambda b, i: (b, i, 0))
    return pl.pallas_call(
        _oproj_kernel,
        grid=(bsz, s // tm),
        in_specs=[pl.BlockSpec((None, d, tm), lambda b, i: (b, 0, i)), row_spec,
                  _const_spec((d, d)), _const_spec((1, d)), _const_spec((1, d))],
        out_specs=row_spec,
        out_shape=jax.ShapeDtypeStruct(x.shape, F32),
        compiler_params=_params(("arbitrary", "arbitrary")),
        name="attn_oproj",
    )(ot, x, wo, ln_g, ln_b)


def _attn_bias_table(rel_bias):
    kj = jnp.arange(ATT_NK)[:, None]
    qi = jnp.arange(ATT_TQ)[None, :]
    rel = jnp.clip(ATT_HIST + qi - kj, -MAX_REL_DIST, MAX_REL_DIST) + MAX_REL_DIST
    chunk_gap = qi // CHUNK - (kj // CHUNK - LEFT_CHUNKS)
    visible = (chunk_gap >= 0) & (chunk_gap <= LEFT_CHUNKS)
    return jnp.where(visible[None], rel_bias[:, rel].astype(F32), NEG_INF)


def _row(v):
    return v.reshape(1, -1)


def kernel(x, p, mix_w_in, pool_w, pool_scale, conv_dw_w, conv_dw_b, conv_ln_g, conv_ln_b,
           mix_w_out, attn_w_qkv, attn_rel_bias, attn_w_o, ln_mix_g, ln_mix_b, ffn_w_up,
           ffn_dw_w, ffn_dw_b, ffn_w_down, ple_w_proj, ple_w_gate, ple_b_gate, ln_ffn_g,
           ln_ffn_b):
    def ffn(xi, i):
        return _ffn(xi, p, i, ffn_w_up[i].astype(BF16), ffn_dw_w[i], _row(ffn_dw_b[i]),
                    ffn_w_down[i].astype(BF16), ple_w_proj[i].astype(BF16),
                    ple_w_gate[i].astype(BF16), _row(ple_b_gate[i]),
                    _row(ln_ffn_g[i]), _row(ln_ffn_b[i]))

    x = _mixer(x, mix_w_in[0].astype(BF16), pool_w[0].astype(BF16), _row(pool_scale[0]),
               conv_dw_w[0], _row(conv_dw_b[0]), _row(conv_ln_g[0]), _row(conv_ln_b[0]),
               mix_w_out[0].astype(BF16), _row(ln_mix_g[0]), _row(ln_mix_b[0]))
    x = ffn(x, 0)

    w_qkv = attn_w_qkv[0]
    wqt = (w_qkv[:, :D_MODEL].T * (HEAD_DIM ** -0.5)).astype(BF16)
    wk = w_qkv[:, D_MODEL:2 * D_MODEL].astype(BF16)
    wvt = w_qkv[:, 2 * D_MODEL:].T.astype(BF16)
    qt, k, vt = _qkv(x, wqt, wk, wvt)
    ot = _attention(qt, k, vt, _attn_bias_table(attn_rel_bias[0]))
    x = _oproj(ot, x, attn_w_o[0].astype(BF16), _row(ln_mix_g[1]), _row(ln_mix_b[1]))
    x = ffn(x, 1)
    return x
```

```python
import functools

import jax
import jax.numpy as jnp
from jax import lax
from jax.experimental import pallas as pl
from jax.experimental.pallas import tpu as pltpu

F32 = jnp.float32
BF16 = jnp.bfloat16

D_MODEL = 1024
DEPTH = 2
CHUNK = 64
PLE_DIM = 256
D_POOL = 512
D_CONV = 512
POOL_WINDOWS = (2, 4, 8, 16)
POOL_GROUP = 128
CONV_KERNEL = 31
HEAD_DIM = 64
N_HEADS = 16
LEFT_CHUNKS = 8
MAX_REL_DIST = 256
D_FF = 2816
FFN_CONV_KERNEL = 3
DEEPNORM_ALPHA = (2 * DEPTH) ** 0.25
LN_EPS = 1e-5
NEG_INF = -1e30

SUBLANES = 8
LANES = 128
VMEM_LIMIT_BYTES = 56 * 1024 * 1024

MIX_TM = 512
MIX_RC = 32
POOL_HALO = 16
CONV_HALO = 32
FFN_TM = 512
FFN_CH = 256
FFN_HALO = SUBLANES
QKV_TM = 512
ATT_TQ = 256
ATT_HIST = LEFT_CHUNKS * CHUNK
ATT_NK = ATT_HIST + ATT_TQ
HEADS_PER_STEP = LANES // HEAD_DIM
OPROJ_TM = 512


def _layer_norm(v, g, b):
    mu = jnp.mean(v, axis=-1, keepdims=True)
    c = v - mu
    var = jnp.mean(c * c, axis=-1, keepdims=True)
    return c * lax.rsqrt(var + LN_EPS) * g + b


def _params(semantics):
    return pltpu.CompilerParams(dimension_semantics=semantics,
                                vmem_limit_bytes=VMEM_LIMIT_BYTES)


def _const_spec(shape):
    zeros = (0,) * len(shape)
    return pl.BlockSpec(shape, lambda *_: zeros, pipeline_mode=pl.Buffered(1))


def _mixer_kernel(x_ref, w_in_ref, pool_w_ref, pool_scale_ref, dw_w_ref, dw_b_ref,
                  cn_g_ref, cn_b_ref, w_out_ref, ln_g_ref, ln_b_ref, o_ref,
                  a_ext, g_ext, g_ph, cat):
    i = pl.program_id(1)
    tm = x_ref.shape[0]

    @pl.when(i == 0)
    def _():
        a_ext[0:POOL_HALO, :] = jnp.zeros((POOL_HALO, D_POOL), F32)
        g_ext[0:CONV_HALO, :] = jnp.zeros((CONV_HALO, D_CONV), F32)

    @pl.when(i > 0)
    def _():
        a_ext[0:POOL_HALO, :] = a_ext[tm:tm + POOL_HALO, :]
        g_ext[0:CONV_HALO, :] = g_ext[tm:tm + CONV_HALO, :]

    xb = x_ref[...]
    u = jnp.dot(xb.astype(BF16), w_in_ref[...], preferred_element_type=F32)
    a_ext[POOL_HALO:, :] = u[:, :D_POOL]
    g_ext[CONV_HALO:, :] = (u[:, D_POOL:D_POOL + D_CONV]
                            * jax.nn.sigmoid(u[:, D_POOL + D_CONV:]))

    row = lax.broadcasted_iota(jnp.int32, (tm, POOL_GROUP), 0)
    pos = (row + (i * tm + 1)).astype(F32)
    for g, w in enumerate(POOL_WINDOWS):
        cols = slice(g * POOL_GROUP, (g + 1) * POOL_GROUP)
        cur = a_ext[POOL_HALO:POOL_HALO + tm, cols]
        s = cur
        for j in range(1, w):
            s = s + a_ext[POOL_HALO - j:POOL_HALO - j + tm, cols]
        d = s / jnp.minimum(pos, float(w)) - cur
        ya = jnp.dot(d.astype(BF16), pool_w_ref[g], preferred_element_type=F32)
        cat[:, cols] = (ya * pool_scale_ref[:, cols]).astype(BF16)

    for p in range(1, SUBLANES):
        g_ph[p - 1, :, :] = g_ext[p:p + g_ph.shape[1], :]

    def conv_chunk(c, carry):
        r0 = pl.multiple_of(c * MIX_RC, MIX_RC)
        acc = None
        for k in range(CONV_KERNEL):
            off = CONV_HALO - (CONV_KERNEL - 1) + k
            q, p = divmod(off, SUBLANES)
            if p == 0:
                t = g_ext[pl.ds(r0 + SUBLANES * q, MIX_RC), :]
            else:
                t = g_ph[p - 1, pl.ds(r0 + SUBLANES * q, MIX_RC), :]
            t = t * dw_w_ref[k:k + 1, :]
            acc = t if acc is None else acc + t
        h = _layer_norm(acc + dw_b_ref[...], cn_g_ref[...], cn_b_ref[...])
        cat[pl.ds(r0, MIX_RC), D_POOL:] = (h * jax.nn.sigmoid(h)).astype(BF16)
        return carry

    lax.fori_loop(0, tm // MIX_RC, conv_chunk, 0)

    mix = jnp.dot(cat[...], w_out_ref[...], preferred_element_type=F32)
    o_ref[...] = _layer_norm(DEEPNORM_ALPHA * xb + mix, ln_g_ref[...], ln_b_ref[...])


def _mixer(x, w_in, pool_w, pool_scale, dw_w, dw_b, cn_g, cn_b, w_out, ln_g, ln_b):
    bsz, s, d = x.shape
    tm = MIX_TM
    d_in = w_in.shape[1]
    return pl.pallas_call(
        _mixer_kernel,
        grid=(bsz, s // tm),
        in_specs=[
            pl.BlockSpec((None, tm, d), lambda b, i: (b, i, 0)),
            _const_spec((d, d_in)),
            _const_spec(pool_w.shape),
            _const_spec((1, D_POOL)),
            _const_spec((CONV_KERNEL, D_CONV)),
            _const_spec((1, D_CONV)),
            _const_spec((1, D_CONV)),
            _const_spec((1, D_CONV)),
            _const_spec((d, d)),
            _const_spec((1, d)),
            _const_spec((1, d)),
        ],
        out_specs=pl.BlockSpec((None, tm, d), lambda b, i: (b, i, 0)),
        out_shape=jax.ShapeDtypeStruct(x.shape, F32),
        scratch_shapes=[
            pltpu.VMEM((tm + POOL_HALO, D_POOL), F32),
            pltpu.VMEM((tm + CONV_HALO, D_CONV), F32),
            pltpu.VMEM((SUBLANES - 1, tm + CONV_HALO - SUBLANES, D_CONV), F32),
            pltpu.VMEM((tm, d), BF16),
        ],
        compiler_params=_params(("arbitrary", "arbitrary")),
        name="mixer0",
    )(x, w_in, pool_w, pool_scale, dw_w, dw_b, cn_g, cn_b, w_out, ln_g, ln_b)


def _ffn_kernel(x_ref, p_ref, wup_ref, dw_w_ref, dw_b_ref, wd_ref,
                wproj_ref, wpg_ref, bpg_ref, ln_g_ref, ln_b_ref, o_ref,
                carry, g_ext):
    i = pl.program_id(1)
    tm = x_ref.shape[0]
    ch = FFN_CH

    @pl.when(i == 0)
    def _():
        carry[...] = jnp.zeros(carry.shape, F32)

    xb = x_ref[...]
    xb16 = xb.astype(BF16)
    acc = None
    for c in range(D_FF // ch):
        cols = slice(c * ch, (c + 1) * ch)
        up = jnp.dot(xb16, wup_ref[:, 2 * c * ch:2 * (c + 1) * ch],
                     preferred_element_type=F32)
        gate, val = up[:, :ch], up[:, ch:]
        g_ext[0:FFN_HALO, cols] = carry[:, cols]
        g_ext[FFN_HALO:, cols] = gate
        carry[:, cols] = gate[tm - FFN_HALO:, :]
        conv = dw_b_ref[:, cols] + gate * dw_w_ref[FFN_CONV_KERNEL - 1:FFN_CONV_KERNEL, cols]
        for k in range(FFN_CONV_KERNEL - 1):
            back = FFN_CONV_KERNEL - 1 - k
            conv = conv + (g_ext[FFN_HALO - back:FFN_HALO - back + tm, cols]
                           * dw_w_ref[k:k + 1, cols])
        act = (jax.nn.gelu(conv) * val).astype(BF16)
        part = jnp.dot(act, wd_ref[cols, :], preferred_element_type=F32)
        acc = part if acc is None else acc + part

    gate_p = jax.nn.sigmoid(
        jnp.dot(xb16, wpg_ref[...], preferred_element_type=F32) + bpg_ref[...])
    ple = gate_p * jnp.dot(p_ref[...].astype(BF16), wproj_ref[...],
                           preferred_element_type=F32)
    y = DEEPNORM_ALPHA * xb + acc + ple
    o_ref[...] = _layer_norm(y, ln_g_ref[...], ln_b_ref[...])


def _ffn(x, p, layer, w_up, dw_w, dw_b, w_down, w_proj, w_pgate, b_pgate, ln_g, ln_b):
    bsz, s, d = x.shape
    tm = FFN_TM
    return pl.pallas_call(
        _ffn_kernel,
        grid=(bsz, s // tm),
        in_specs=[
            pl.BlockSpec((None, tm, d), lambda b, i: (b, i, 0)),
            pl.BlockSpec((None, None, tm, PLE_DIM), lambda b, i: (layer, b, i, 0)),
            _const_spec((d, 2 * D_FF)),
            _const_spec((FFN_CONV_KERNEL, D_FF)),
            _const_spec((1, D_FF)),
            _const_spec((D_FF, d)),
            _const_spec((PLE_DIM, d)),
            _const_spec((d, d)),
            _const_spec((1, d)),
            _const_spec((1, d)),
            _const_spec((1, d)),
        ],
        out_specs=pl.BlockSpec((None, tm, d), lambda b, i: (b, i, 0)),
        out_shape=jax.ShapeDtypeStruct(x.shape, F32),
        scratch_shapes=[
            pltpu.VMEM((FFN_HALO, D_FF), F32),
            pltpu.VMEM((tm + FFN_HALO, D_FF), F32),
        ],
        compiler_params=_params(("arbitrary", "arbitrary")),
        name="convffn",
    )(x, p, w_up, dw_w, dw_b, w_down, w_proj, w_pgate, b_pgate, ln_g, ln_b)


_NT = (((1,), (1,)), ((), ()))
_TN = (((0,), (0,)), ((), ()))


def _qkv_kernel(x_ref, wqt_ref, wk_ref, wvt_ref, qt_ref, k_ref, vt_ref):
    xb16 = x_ref[...].astype(BF16)
    qt_ref[...] = lax.dot_general(wqt_ref[...], xb16, _NT,
                                  preferred_element_type=F32).astype(BF16)
    k_ref[...] = jnp.dot(xb16, wk_ref[...], preferred_element_type=F32).astype(BF16)
    vt_ref[...] = lax.dot_general(wvt_ref[...], xb16, _NT,
                                  preferred_element_type=F32).astype(BF16)


def _qkv(x, wqt, wk, wvt):
    bsz, s, d = x.shape
    tm = QKV_TM
    row_spec = pl.BlockSpec((None, tm, d), lambda b, i: (b, i, 0))
    col_spec = pl.BlockSpec((None, d, tm), lambda b, i: (b, 0, i))
    return pl.pallas_call(
        _qkv_kernel,
        grid=(bsz, s // tm),
        in_specs=[row_spec, _const_spec((d, d)), _const_spec((d, d)), _const_spec((d, d))],
        out_specs=[col_spec, row_spec, col_spec],
        out_shape=[jax.ShapeDtypeStruct((bsz, d, s), BF16),
                   jax.ShapeDtypeStruct((bsz, s, d), BF16),
                   jax.ShapeDtypeStruct((bsz, d, s), BF16)],
        compiler_params=_params(("arbitrary", "arbitrary")),
        name="qkv",
    )(x, wqt, wk, wvt)


def _attn_kernel(qt_ref, k_ref, vt_ref, bias_ref, ot_ref):
    s_len = k_ref.shape[0]
    tq = ATT_TQ
    head_of_row = lax.broadcasted_iota(jnp.int32, (LANES, tq), 0) // HEAD_DIM

    def block(q0, nk):
        k0 = q0 + tq - nk
        if not isinstance(k0, int):
            k0 = pl.multiple_of(k0, tq)
        kwin = k_ref[pl.ds(k0, nk), :]
        qblk = qt_ref[:, pl.ds(q0, tq)]
        for h in range(HEADS_PER_STEP):
            qh = jnp.where(head_of_row == h, qblk, jnp.zeros_like(qblk))
            sc = jnp.dot(kwin, qh, preferred_element_type=F32)
            sc = sc + bias_ref[h, ATT_NK - nk:, :]
            m = jnp.max(sc, axis=0, keepdims=True)
            pr = jnp.exp(sc - m)
            denom = jnp.sum(pr, axis=0, keepdims=True)
            vwin = vt_ref[h * HEAD_DIM:(h + 1) * HEAD_DIM, pl.ds(k0, nk)]
            o = jnp.dot(vwin, pr.astype(BF16), preferred_element_type=F32)
            ot_ref[h * HEAD_DIM:(h + 1) * HEAD_DIM, pl.ds(q0, tq)] = (o / denom).astype(BF16)

    n_short = ATT_HIST // tq
    for b in range(n_short):
        block(b * tq, (b + 1) * tq)

    def body(b, carry):
        block(pl.multiple_of(b * tq, tq), ATT_NK)
        return carry

    lax.fori_loop(n_short, s_len // tq, body, 0)


def _attention(qt, k, vt, bias_t):
    bsz, s, d = k.shape
    n_groups = d // LANES
    lane_spec = pl.BlockSpec((None, LANES, s), lambda g, b: (b, g, 0))
    return pl.pallas_call(
        _attn_kernel,
        grid=(n_groups, bsz),
        in_specs=[
            lane_spec,
            pl.BlockSpec((None, s, LANES), lambda g, b: (b, 0, g)),
            lane_spec,
            pl.BlockSpec((HEADS_PER_STEP, ATT_NK, ATT_TQ), lambda g, b: (g, 0, 0)),
        ],
        out_specs=lane_spec,
        out_shape=jax.ShapeDtypeStruct((bsz, d, s), BF16),
        compiler_params=_params(("arbitrary", "arbitrary")),
        name="band_attn",
    )(qt, k, vt, bias_t)


def _oproj_kernel(ot_ref, x_ref, wo_ref, ln_g_ref, ln_b_ref, o_ref):
    mix = lax.dot_general(ot_ref[...], wo_ref[...], _TN, preferred_element_type=F32)
    o_ref[...] = _layer_norm(DEEPNORM_ALPHA * x_ref[...] + mix, ln_g_ref[...], ln_b_ref[...])


def _oproj(ot, x, wo, ln_g, ln_b):
    bsz, s, d = x.shape
    tm = OPROJ_TM
    row_spec = pl.BlockSpec((None, tm, d), lambda b, i: (b, i, 0))
    return pl.pallas_call(
        _oproj_kernel,
        grid=(bsz, s // tm),
        in_specs=[pl.BlockSpec((None, d, tm), lambda b, i: (b, 0, i)), row_spec,
                  _const_spec((d, d)), _const_spec((1, d)), _const_spec((1, d))],
        out_specs=row_spec,
        out_shape=jax.ShapeDtypeStruct(x.shape, F32),
        compiler_params=_params(("arbitrary", "arbitrary")),
        name="attn_oproj",
    )(ot, x, wo, ln_g, ln_b)


def _attn_bias_table(rel_bias):
    n_heads = rel_bias.shape[0]
    length = ATT_NK + ATT_TQ - 1
    dist = jnp.arange(length) - (ATT_NK - 1 - ATT_HIST)
    per_dist = rel_bias[:, jnp.clip(dist, -MAX_REL_DIST, MAX_REL_DIST) + MAX_REL_DIST]
    ext = jnp.pad(per_dist.astype(F32), ((0, 0), (0, 1)))
    skew = jnp.tile(ext, (1, ATT_NK))[:, :ATT_NK * length].reshape(n_heads, ATT_NK, length)
    table = skew[:, :, ATT_NK - 1:]
    kj = jnp.arange(ATT_NK)[:, None]
    qi = jnp.arange(ATT_TQ)[None, :]
    chunk_gap = qi // CHUNK - (kj // CHUNK - LEFT_CHUNKS)
    visible = (chunk_gap >= 0) & (chunk_gap <= LEFT_CHUNKS)
    return jnp.where(visible[None], table, NEG_INF)


def _row(v):
    return v.reshape(1, -1)


def kernel(x, p, mix_w_in, pool_w, pool_scale, conv_dw_w, conv_dw_b, conv_ln_g, conv_ln_b,
           mix_w_out, attn_w_qkv, attn_rel_bias, attn_w_o, ln_mix_g, ln_mix_b, ffn_w_up,
           ffn_dw_w, ffn_dw_b, ffn_w_down, ple_w_proj, ple_w_gate, ple_b_gate, ln_ffn_g,
           ln_ffn_b):
    def ffn(xi, i):
        n_ch = D_FF // FFN_CH
        w_up = ffn_w_up[i].astype(BF16).reshape(D_MODEL, 2, n_ch, FFN_CH)
        w_up = w_up.transpose(0, 2, 1, 3).reshape(D_MODEL, 2 * D_FF)
        return _ffn(xi, p, i, w_up, ffn_dw_w[i], _row(ffn_dw_b[i]),
                    ffn_w_down[i].astype(BF16), ple_w_proj[i].astype(BF16),
                    ple_w_gate[i].astype(BF16), _row(ple_b_gate[i]),
                    _row(ln_ffn_g[i]), _row(ln_ffn_b[i]))

    x = _mixer(x, mix_w_in[0].astype(BF16), pool_w[0].astype(BF16), _row(pool_scale[0]),
               conv_dw_w[0], _row(conv_dw_b[0]), _row(conv_ln_g[0]), _row(conv_ln_b[0]),
               mix_w_out[0].astype(BF16), _row(ln_mix_g[0]), _row(ln_mix_b[0]))
    x = ffn(x, 0)

    w_qkv = attn_w_qkv[0]
    wqt = (w_qkv[:, :D_MODEL].T * (HEAD_DIM ** -0.5)).astype(BF16)
    wk = w_qkv[:, D_MODEL:2 * D_MODEL].astype(BF16)
    wvt = w_qkv[:, 2 * D_MODEL:].T.astype(BF16)
    qt, k, vt = _qkv(x, wqt, wk, wvt)
    ot = _attention(qt, k, vt, _attn_bias_table(attn_rel_bias[0]))
    x = _oproj(ot, x, attn_w_o[0].astype(BF16), _row(ln_mix_g[1]), _row(ln_mix_b[1]))
    x = ffn(x, 1)
    return x
```

```python
import functools

import jax
import jax.numpy as jnp
from jax import lax
from jax.experimental import pallas as pl
from jax.experimental.pallas import tpu as pltpu

F32 = jnp.float32
BF16 = jnp.bfloat16

D_MODEL = 1024
DEPTH = 2
CHUNK = 64
PLE_DIM = 256
D_POOL = 512
D_CONV = 512
POOL_WINDOWS = (2, 4, 8, 16)
POOL_GROUP = 128
CONV_KERNEL = 31
HEAD_DIM = 64
N_HEADS = 16
LEFT_CHUNKS = 8
MAX_REL_DIST = 256
D_FF = 2816
FFN_CONV_KERNEL = 3
DEEPNORM_ALPHA = (2 * DEPTH) ** 0.25
LN_EPS = 1e-5
NEG_INF = -1e30

SUBLANES = 8
LANES = 128
VMEM_LIMIT_BYTES = 56 * 1024 * 1024

MIX_TM = 512
MIX_RC = 32
POOL_HALO = 16
CONV_HALO = 32
FFN_TM = 512
FFN_CH = 256
FFN_HALO = SUBLANES
QKV_TM = 512
ATT_TQ = 256
ATT_HIST = LEFT_CHUNKS * CHUNK
ATT_NK = ATT_HIST + ATT_TQ
HEADS_PER_STEP = LANES // HEAD_DIM
OPROJ_TM = 512


def _layer_norm(v, g, b):
    mu = jnp.mean(v, axis=-1, keepdims=True)
    c = v - mu
    var = jnp.mean(c * c, axis=-1, keepdims=True)
    return c * lax.rsqrt(var + LN_EPS) * g + b


def _params(semantics):
    return pltpu.CompilerParams(dimension_semantics=semantics,
                                vmem_limit_bytes=VMEM_LIMIT_BYTES)


def _const_spec(shape):
    zeros = (0,) * len(shape)
    return pl.BlockSpec(shape, lambda *_: zeros, pipeline_mode=pl.Buffered(1))


def _mixer_kernel(x_ref, w_in_ref, pool_w_ref, pool_scale_ref, dw_w_ref, dw_b_ref,
                  cn_g_ref, cn_b_ref, w_out_ref, ln_g_ref, ln_b_ref, o_ref,
                  a_ext, g_ext, g_ph, conv_out, cat):
    i = pl.program_id(1)
    tm = x_ref.shape[0]

    @pl.when(i == 0)
    def _():
        a_ext[0:POOL_HALO, :] = jnp.zeros((POOL_HALO, D_POOL), F32)
        g_ext[0:CONV_HALO, :] = jnp.zeros((CONV_HALO, D_CONV), F32)

    @pl.when(i > 0)
    def _():
        a_ext[0:POOL_HALO, :] = a_ext[tm:tm + POOL_HALO, :]
        g_ext[0:CONV_HALO, :] = g_ext[tm:tm + CONV_HALO, :]

    xb = x_ref[...]
    u = jnp.dot(xb.astype(BF16), w_in_ref[...], preferred_element_type=F32)
    a_ext[POOL_HALO:, :] = u[:, :D_POOL]
    g_ext[CONV_HALO:, :] = (u[:, D_POOL:D_POOL + D_CONV]
                            * jax.nn.sigmoid(u[:, D_POOL + D_CONV:]))

    row = lax.broadcasted_iota(jnp.int32, (tm, POOL_GROUP), 0)
    pos = (row + (i * tm + 1)).astype(F32)
    for g, w in enumerate(POOL_WINDOWS):
        cols = slice(g * POOL_GROUP, (g + 1) * POOL_GROUP)
        cur = a_ext[POOL_HALO:POOL_HALO + tm, cols]
        s = cur
        for j in range(1, w):
            s = s + a_ext[POOL_HALO - j:POOL_HALO - j + tm, cols]
        d = s / jnp.minimum(pos, float(w)) - cur
        ya = jnp.dot(d.astype(BF16), pool_w_ref[g], preferred_element_type=F32)
        cat[:, cols] = (ya * pool_scale_ref[:, cols]).astype(BF16)

    for p in range(1, SUBLANES):
        g_ph[p - 1, :, :] = g_ext[p:p + g_ph.shape[1], :]

    def conv_chunk(c, carry):
        r0 = pl.multiple_of(c * MIX_RC, MIX_RC)
        acc = None
        for k in range(CONV_KERNEL):
            off = CONV_HALO - (CONV_KERNEL - 1) + k
            q, p = divmod(off, SUBLANES)
            if p == 0:
                t = g_ext[pl.ds(r0 + SUBLANES * q, MIX_RC), :]
            else:
                t = g_ph[p - 1, pl.ds(r0 + SUBLANES * q, MIX_RC), :]
            t = t * dw_w_ref[k:k + 1, :]
            acc = t if acc is None else acc + t
        conv_out[pl.ds(r0, MIX_RC), :] = acc
        return carry

    lax.fori_loop(0, tm // MIX_RC, conv_chunk, 0)
    h = _layer_norm(conv_out[...] + dw_b_ref[...], cn_g_ref[...], cn_b_ref[...])
    cat[:, D_POOL:] = (h * jax.nn.sigmoid(h)).astype(BF16)

    mix = jnp.dot(cat[...], w_out_ref[...], preferred_element_type=F32)
    o_ref[...] = _layer_norm(DEEPNORM_ALPHA * xb + mix, ln_g_ref[...], ln_b_ref[...])


def _mixer(x, w_in, pool_w, pool_scale, dw_w, dw_b, cn_g, cn_b, w_out, ln_g, ln_b):
    bsz, s, d = x.shape
    tm = MIX_TM
    d_in = w_in.shape[1]
    return pl.pallas_call(
        _mixer_kernel,
        grid=(bsz, s // tm),
        in_specs=[
            pl.BlockSpec((None, tm, d), lambda b, i: (b, i, 0)),
            _const_spec((d, d_in)),
            _const_spec(pool_w.shape),
            _const_spec((1, D_POOL)),
            _const_spec((CONV_KERNEL, D_CONV)),
            _const_spec((1, D_CONV)),
            _const_spec((1, D_CONV)),
            _const_spec((1, D_CONV)),
            _const_spec((d, d)),
            _const_spec((1, d)),
            _const_spec((1, d)),
        ],
        out_specs=pl.BlockSpec((None, tm, d), lambda b, i: (b, i, 0)),
        out_shape=jax.ShapeDtypeStruct(x.shape, F32),
        scratch_shapes=[
            pltpu.VMEM((tm + POOL_HALO, D_POOL), F32),
            pltpu.VMEM((tm + CONV_HALO, D_CONV), F32),
            pltpu.VMEM((SUBLANES - 1, tm + CONV_HALO - SUBLANES, D_CONV), F32),
            pltpu.VMEM((tm, D_CONV), F32),
            pltpu.VMEM((tm, d), BF16),
        ],
        compiler_params=_params(("arbitrary", "arbitrary")),
        name="mixer0",
    )(x, w_in, pool_w, pool_scale, dw_w, dw_b, cn_g, cn_b, w_out, ln_g, ln_b)


def _ffn_kernel(x_ref, p_ref, wup_ref, dw_w_ref, dw_b_ref, wd_ref,
                wproj_ref, wpg_ref, bpg_ref, ln_g_ref, ln_b_ref, o_ref,
                carry, g_ext):
    i = pl.program_id(1)
    tm = x_ref.shape[0]
    ch = FFN_CH

    @pl.when(i == 0)
    def _():
        carry[...] = jnp.zeros(carry.shape, F32)

    xb = x_ref[...]
    xb16 = xb.astype(BF16)
    acc = None
    for c in range(D_FF // ch):
        cols = slice(c * ch, (c + 1) * ch)
        gate = jnp.dot(xb16, wup_ref[:, cols], preferred_element_type=F32)
        val = jnp.dot(xb16, wup_ref[:, D_FF + c * ch:D_FF + (c + 1) * ch],
                      preferred_element_type=F32)
        g_ext[0:FFN_HALO, cols] = carry[:, cols]
        g_ext[FFN_HALO:, cols] = gate
        carry[:, cols] = gate[tm - FFN_HALO:, :]
        conv = dw_b_ref[:, cols] + gate * dw_w_ref[FFN_CONV_KERNEL - 1:FFN_CONV_KERNEL, cols]
        for k in range(FFN_CONV_KERNEL - 1):
            back = FFN_CONV_KERNEL - 1 - k
            conv = conv + (g_ext[FFN_HALO - back:FFN_HALO - back + tm, cols]
                           * dw_w_ref[k:k + 1, cols])
        act = (jax.nn.gelu(conv) * val).astype(BF16)
        part = jnp.dot(act, wd_ref[cols, :], preferred_element_type=F32)
        acc = part if acc is None else acc + part

    gate_p = jax.nn.sigmoid(
        jnp.dot(xb16, wpg_ref[...], preferred_element_type=F32) + bpg_ref[...])
    ple = gate_p * jnp.dot(p_ref[...].astype(BF16), wproj_ref[...],
                           preferred_element_type=F32)
    y = DEEPNORM_ALPHA * xb + acc + ple
    o_ref[...] = _layer_norm(y, ln_g_ref[...], ln_b_ref[...])


def _ffn(x, p, layer, w_up, dw_w, dw_b, w_down, w_proj, w_pgate, b_pgate, ln_g, ln_b):
    bsz, s, d = x.shape
    tm = FFN_TM
    return pl.pallas_call(
        _ffn_kernel,
        grid=(bsz, s // tm),
        in_specs=[
            pl.BlockSpec((None, tm, d), lambda b, i: (b, i, 0)),
            pl.BlockSpec((None, None, tm, PLE_DIM), lambda b, i: (layer, b, i, 0)),
            _const_spec((d, 2 * D_FF)),
            _const_spec((FFN_CONV_KERNEL, D_FF)),
            _const_spec((1, D_FF)),
            _const_spec((D_FF, d)),
            _const_spec((PLE_DIM, d)),
            _const_spec((d, d)),
            _const_spec((1, d)),
            _const_spec((1, d)),
            _const_spec((1, d)),
        ],
        out_specs=pl.BlockSpec((None, tm, d), lambda b, i: (b, i, 0)),
        out_shape=jax.ShapeDtypeStruct(x.shape, F32),
        scratch_shapes=[
            pltpu.VMEM((FFN_HALO, D_FF), F32),
            pltpu.VMEM((tm + FFN_HALO, D_FF), F32),
        ],
        compiler_params=_params(("arbitrary", "arbitrary")),
        name="convffn",
    )(x, p, w_up, dw_w, dw_b, w_down, w_proj, w_pgate, b_pgate, ln_g, ln_b)


_NT = (((1,), (1,)), ((), ()))
_TN = (((0,), (0,)), ((), ()))


def _qkv_kernel(x_ref, wqt_ref, wk_ref, wvt_ref, qt_ref, k_ref, vt_ref):
    xb16 = x_ref[...].astype(BF16)
    qt_ref[...] = lax.dot_general(wqt_ref[...], xb16, _NT,
                                  preferred_element_type=F32).astype(BF16)
    k_ref[...] = jnp.dot(xb16, wk_ref[...], preferred_element_type=F32).astype(BF16)
    vt_ref[...] = lax.dot_general(wvt_ref[...], xb16, _NT,
                                  preferred_element_type=F32).astype(BF16)


def _qkv(x, wqt, wk, wvt):
    bsz, s, d = x.shape
    tm = QKV_TM
    row_spec = pl.BlockSpec((None, tm, d), lambda b, i: (b, i, 0))
    col_spec = pl.BlockSpec((None, d, tm), lambda b, i: (b, 0, i))
    return pl.pallas_call(
        _qkv_kernel,
        grid=(bsz, s // tm),
        in_specs=[row_spec, _const_spec((d, d)), _const_spec((d, d)), _const_spec((d, d))],
        out_specs=[col_spec, row_spec, col_spec],
        out_shape=[jax.ShapeDtypeStruct((bsz, d, s), BF16),
                   jax.ShapeDtypeStruct((bsz, s, d), BF16),
                   jax.ShapeDtypeStruct((bsz, d, s), BF16)],
        compiler_params=_params(("arbitrary", "arbitrary")),
        name="qkv",
    )(x, wqt, wk, wvt)


def _attn_kernel(qt_ref, k_ref, vt_ref, bias_ref, ot_ref, s_a, s_b):
    s_len = k_ref.shape[0]
    tq = ATT_TQ
    n_kt, n_qt = ATT_NK // LANES, tq // LANES
    head_of_row = lax.broadcasted_iota(jnp.int32, (LANES, tq), 0) // HEAD_DIM

    def window_start(q0, nk):
        k0 = q0 + tq - nk
        return k0 if isinstance(k0, int) else pl.multiple_of(k0, tq)

    def scores(q0, nk, h):
        kwin = k_ref[pl.ds(window_start(q0, nk), nk), :]
        qblk = qt_ref[:, pl.ds(q0, tq)]
        qh = jnp.where(head_of_row == h, qblk, jnp.zeros_like(qblk))
        first_kt = n_kt - nk // LANES
        bias = jnp.concatenate([
            jnp.concatenate([
                bias_ref[h, :, (qt - kt + n_kt - 1) * LANES:(qt - kt + n_kt) * LANES]
                for qt in range(n_qt)], axis=1)
            for kt in range(first_kt, n_kt)], axis=0)
        return jnp.dot(kwin, qh, preferred_element_type=F32) + bias

    def softmax_pv(q0, nk, h, sc):
        m = jnp.max(sc, axis=0, keepdims=True)
        pr = jnp.exp(sc - m)
        denom = jnp.sum(pr, axis=0, keepdims=True)
        vwin = vt_ref[h * HEAD_DIM:(h + 1) * HEAD_DIM, pl.ds(window_start(q0, nk), nk)]
        o = jnp.dot(vwin, pr.astype(BF16), preferred_element_type=F32)
        ot_ref[h * HEAD_DIM:(h + 1) * HEAD_DIM, pl.ds(q0, tq)] = (o / denom).astype(BF16)

    n_short = ATT_HIST // tq
    for b in range(n_short):
        for h in range(HEADS_PER_STEP):
            softmax_pv(b * tq, (b + 1) * tq, h, scores(b * tq, (b + 1) * tq, h))

    n_blocks = s_len // tq
    s_a[...] = scores(n_short * tq, ATT_NK, 0)

    def body(b, carry):
        q0 = pl.multiple_of(b * tq, tq)
        q_next = pl.multiple_of(jnp.minimum(b + 1, n_blocks - 1) * tq, tq)
        s_b[...] = scores(q0, ATT_NK, 1)
        softmax_pv(q0, ATT_NK, 0, s_a[...])
        s_a[...] = scores(q_next, ATT_NK, 0)
        softmax_pv(q0, ATT_NK, 1, s_b[...])
        return carry

    lax.fori_loop(n_short, n_blocks, body, 0)


def _attention(qt, k, vt, bias_t):
    bsz, s, d = k.shape
    n_groups = d // LANES
    lane_spec = pl.BlockSpec((None, LANES, s), lambda g, b: (b, g, 0))
    return pl.pallas_call(
        _attn_kernel,
        grid=(n_groups, bsz),
        in_specs=[
            lane_spec,
            pl.BlockSpec((None, s, LANES), lambda g, b: (b, 0, g)),
            lane_spec,
            pl.BlockSpec((HEADS_PER_STEP,) + bias_t.shape[1:], lambda g, b: (g, 0, 0)),
        ],
        out_specs=lane_spec,
        out_shape=jax.ShapeDtypeStruct((bsz, d, s), BF16),
        scratch_shapes=[pltpu.VMEM((ATT_NK, ATT_TQ), F32), pltpu.VMEM((ATT_NK, ATT_TQ), F32)],
        compiler_params=_params(("arbitrary", "arbitrary")),
        name="band_attn",
    )(qt, k, vt, bias_t)


def _oproj_kernel(ot_ref, x_ref, wo_ref, ln_g_ref, ln_b_ref, o_ref):
    mix = lax.dot_general(ot_ref[...], wo_ref[...], _TN, preferred_element_type=F32)
    o_ref[...] = _layer_norm(DEEPNORM_ALPHA * x_ref[...] + mix, ln_g_ref[...], ln_b_ref[...])


def _oproj(ot, x, wo, ln_g, ln_b):
    bsz, s, d = x.shape
    tm = OPROJ_TM
    row_spec = pl.BlockSpec((None, tm, d), lambda b, i: (b, i, 0))
    return pl.pallas_call(
        _oproj_kernel,
        grid=(bsz, s // tm),
        in_specs=[pl.BlockSpec((None, d, tm), lambda b, i: (b, 0, i)), row_spec,
                  _const_spec((d, d)), _const_spec((1, d)), _const_spec((1, d))],
        out_specs=row_spec,
        out_shape=jax.ShapeDtypeStruct(x.shape, F32),
        compiler_params=_params(("arbitrary", "arbitrary")),
        name="attn_oproj",
    )(ot, x, wo, ln_g, ln_b)


def _attn_bias_table(rel_bias):
    n_heads = rel_bias.shape[0]
    n_m = (ATT_NK + ATT_TQ) // LANES - 1
    length = (n_m + 1) * LANES - 1
    dist = jnp.arange(length) - (LANES - 1) - (ATT_NK - LANES - ATT_HIST)
    per_dist = rel_bias[:, jnp.clip(dist, -MAX_REL_DIST, MAX_REL_DIST) + MAX_REL_DIST]
    ext = jnp.pad(per_dist.astype(F32), ((0, 0), (0, 1)))
    skew = jnp.tile(ext, (1, LANES))[:, :LANES * length].reshape(n_heads, LANES, length)
    table = skew[:, :, LANES - 1:]
    a = jnp.arange(LANES)[:, None]
    c = jnp.arange(n_m * LANES)[None, :]
    m, b = c // LANES, c % LANES
    chunk_gap = ((m - (ATT_NK // LANES - 1)) * (LANES // CHUNK) + LEFT_CHUNKS
                 + b // CHUNK - a // CHUNK)
    visible = (chunk_gap >= 0) & (chunk_gap <= LEFT_CHUNKS)
    return jnp.where(visible[None], table, NEG_INF)


def _row(v):
    return v.reshape(1, -1)


def kernel(x, p, mix_w_in, pool_w, pool_scale, conv_dw_w, conv_dw_b, conv_ln_g, conv_ln_b,
           mix_w_out, attn_w_qkv, attn_rel_bias, attn_w_o, ln_mix_g, ln_mix_b, ffn_w_up,
           ffn_dw_w, ffn_dw_b, ffn_w_down, ple_w_proj, ple_w_gate, ple_b_gate, ln_ffn_g,
           ln_ffn_b):
    def ffn(xi, i):
        return _ffn(xi, p, i, ffn_w_up[i].astype(BF16), ffn_dw_w[i], _row(ffn_dw_b[i]),
                    ffn_w_down[i].astype(BF16), ple_w_proj[i].astype(BF16),
                    ple_w_gate[i].astype(BF16), _row(ple_b_gate[i]),
                    _row(ln_ffn_g[i]), _row(ln_ffn_b[i]))

    x = _mixer(x, mix_w_in[0].astype(BF16), pool_w[0].astype(BF16), _row(pool_scale[0]),
               conv_dw_w[0], _row(conv_dw_b[0]), _row(conv_ln_g[0]), _row(conv_ln_b[0]),
               mix_w_out[0].astype(BF16), _row(ln_mix_g[0]), _row(ln_mix_b[0]))
    x = ffn(x, 0)

    w_qkv = attn_w_qkv[0]
    wqt = (w_qkv[:, :D_MODEL].T * (HEAD_DIM ** -0.5)).astype(BF16)
    wk = w_qkv[:, D_MODEL:2 * D_MODEL].astype(BF16)
    wvt = w_qkv[:, 2 * D_MODEL:].T.astype(BF16)
    qt, k, vt = _qkv(x, wqt, wk, wvt)
    ot = _attention(qt, k, vt, _attn_bias_table(attn_rel_bias[0]))
    x = _oproj(ot, x, attn_w_o[0].astype(BF16), _row(ln_mix_g[1]), _row(ln_mix_b[1]))
    x = ffn(x, 1)
    return x
```

```python
import functools

import jax
import jax.numpy as jnp
from jax import lax
from jax.experimental import pallas as pl
from jax.experimental.pallas import tpu as pltpu

F32 = jnp.float32
BF16 = jnp.bfloat16

D_MODEL = 1024
DEPTH = 2
CHUNK = 64
PLE_DIM = 256
D_POOL = 512
D_CONV = 512
POOL_WINDOWS = (2, 4, 8, 16)
POOL_GROUP = 128
CONV_KERNEL = 31
HEAD_DIM = 64
N_HEADS = 16
LEFT_CHUNKS = 8
MAX_REL_DIST = 256
D_FF = 2816
FFN_CONV_KERNEL = 3
DEEPNORM_ALPHA = (2 * DEPTH) ** 0.25
LN_EPS = 1e-5
NEG_INF = -1e30
LOG2_E = 1.4426950408889634
GELU_C1 = -2.0 * LOG2_E * 0.7978845608028654
GELU_C3 = GELU_C1 * 0.044715

SUBLANES = 8
BF16_SUBLANES = 16
LANES = 128
VMEM_LIMIT_BYTES = 56 * 1024 * 1024

MIX_TM = 512
MIX_RC = 32
POOL_HALO = 16
CONV_HALO = 32
FFN_TM = 256
FFN_CH = 256
FFN_HALO = SUBLANES
QKV_TM = 512
ATT_TQ = 256
ATT_HIST = LEFT_CHUNKS * CHUNK
ATT_NK = ATT_HIST + ATT_TQ
HEADS_PER_STEP = LANES // HEAD_DIM
ATT_AHEAD = 2
OPROJ_TM = 512


def _layer_norm(v, g, b):
    mu = jnp.mean(v, axis=-1, keepdims=True)
    c = v - mu
    var = jnp.mean(c * c, axis=-1, keepdims=True)
    return c * lax.rsqrt(var + LN_EPS) * g + b


def _gelu_tanh(v):
    u2 = v * (GELU_C1 + GELU_C3 * (v * v))
    return v / (1.0 + jnp.exp2(u2))


def _params(semantics):
    return pltpu.CompilerParams(dimension_semantics=semantics,
                                vmem_limit_bytes=VMEM_LIMIT_BYTES)


def _const_spec(shape):
    zeros = (0,) * len(shape)
    return pl.BlockSpec(shape, lambda *_: zeros, pipeline_mode=pl.Buffered(1))


def _mixer_kernel(x_ref, w_in_ref, pool_w_ref, pool_scale_ref, dw_w_ref, dw_b_ref,
                  cn_g_ref, cn_b_ref, w_out_ref, ln_g_ref, ln_b_ref, o_ref,
                  a_ext, g_ext, g_ph, conv_out, cat):
    i = pl.program_id(1)
    tm = x_ref.shape[0]

    @pl.when(i == 0)
    def _():
        a_ext[0:POOL_HALO, :] = jnp.zeros((POOL_HALO, D_POOL), F32)
        g_ext[0:CONV_HALO, :] = jnp.zeros((CONV_HALO, D_CONV), F32)

    @pl.when(i > 0)
    def _():
        a_ext[0:POOL_HALO, :] = a_ext[tm:tm + POOL_HALO, :]
        g_ext[0:CONV_HALO, :] = g_ext[tm:tm + CONV_HALO, :]

    xb = x_ref[...]
    u = jnp.dot(xb.astype(BF16), w_in_ref[...], preferred_element_type=F32)
    a_ext[POOL_HALO:, :] = u[:, :D_POOL]
    g_ext[CONV_HALO:, :] = (u[:, D_POOL:D_POOL + D_CONV]
                            * jax.nn.sigmoid(u[:, D_POOL + D_CONV:]))

    row = lax.broadcasted_iota(jnp.int32, (tm, POOL_GROUP), 0)
    pos = (row + (i * tm + 1)).astype(F32)
    for g, w in enumerate(POOL_WINDOWS):
        cols = slice(g * POOL_GROUP, (g + 1) * POOL_GROUP)
        cur = a_ext[POOL_HALO:POOL_HALO + tm, cols]
        s = cur
        for j in range(1, w):
            s = s + a_ext[POOL_HALO - j:POOL_HALO - j + tm, cols]
        d = s / jnp.minimum(pos, float(w)) - cur
        ya = jnp.dot(d.astype(BF16), pool_w_ref[g], preferred_element_type=F32)
        cat[:, cols] = (ya * pool_scale_ref[:, cols]).astype(BF16)

    for p in range(1, SUBLANES):
        g_ph[p - 1, :, :] = g_ext[p:p + g_ph.shape[1], :]

    def conv_chunk(c, carry):
        r0 = pl.multiple_of(c * MIX_RC, MIX_RC)
        acc = None
        for k in range(CONV_KERNEL):
            off = CONV_HALO - (CONV_KERNEL - 1) + k
            q, p = divmod(off, SUBLANES)
            if p == 0:
                t = g_ext[pl.ds(r0 + SUBLANES * q, MIX_RC), :]
            else:
                t = g_ph[p - 1, pl.ds(r0 + SUBLANES * q, MIX_RC), :]
            t = t * dw_w_ref[k:k + 1, :]
            acc = t if acc is None else acc + t
        conv_out[pl.ds(r0, MIX_RC), :] = acc
        return carry

    lax.fori_loop(0, tm // MIX_RC, conv_chunk, 0)
    h = _layer_norm(conv_out[...] + dw_b_ref[...], cn_g_ref[...], cn_b_ref[...])
    cat[:, D_POOL:] = (h * jax.nn.sigmoid(h)).astype(BF16)

    mix = jnp.dot(cat[...], w_out_ref[...], preferred_element_type=F32)
    o_ref[...] = _layer_norm(DEEPNORM_ALPHA * xb + mix, ln_g_ref[...], ln_b_ref[...])


def _mixer(x, w_in, pool_w, pool_scale, dw_w, dw_b, cn_g, cn_b, w_out, ln_g, ln_b):
    bsz, s, d = x.shape
    tm = MIX_TM
    d_in = w_in.shape[1]
    return pl.pallas_call(
        _mixer_kernel,
        grid=(bsz, s // tm),
        in_specs=[
            pl.BlockSpec((None, tm, d), lambda b, i: (b, i, 0)),
            _const_spec((d, d_in)),
            _const_spec(pool_w.shape),
            _const_spec((1, D_POOL)),
            _const_spec((CONV_KERNEL, D_CONV)),
            _const_spec((1, D_CONV)),
            _const_spec((1, D_CONV)),
            _const_spec((1, D_CONV)),
            _const_spec((d, d)),
            _const_spec((1, d)),
            _const_spec((1, d)),
        ],
        out_specs=pl.BlockSpec((None, tm, d), lambda b, i: (b, i, 0)),
        out_shape=jax.ShapeDtypeStruct(x.shape, F32),
        scratch_shapes=[
            pltpu.VMEM((tm + POOL_HALO, D_POOL), F32),
            pltpu.VMEM((tm + CONV_HALO, D_CONV), F32),
            pltpu.VMEM((SUBLANES - 1, tm + CONV_HALO - SUBLANES, D_CONV), F32),
            pltpu.VMEM((tm, D_CONV), F32),
            pltpu.VMEM((tm, d), BF16),
        ],
        compiler_params=_params(("arbitrary", "arbitrary")),
        name="mixer0",
    )(x, w_in, pool_w, pool_scale, dw_w, dw_b, cn_g, cn_b, w_out, ln_g, ln_b)


def _ffn_kernel(x_ref, p_ref, wup_ref, dw_w_ref, dw_b_ref, wd_ref,
                wproj_ref, wpg_ref, bpg_ref, ln_g_ref, ln_b_ref, o_ref,
                carry, halo, g_buf, v_buf, act_buf):
    i = pl.program_id(1)
    tm = x_ref.shape[0]
    ch = FFN_CH

    @pl.when(i == 0)
    def _():
        carry[...] = jnp.zeros(carry.shape, F32)

    xb = x_ref[...]
    xb16 = xb.astype(BF16)

    def up_proj(c):
        gate = jnp.dot(xb16, wup_ref[:, c * ch:(c + 1) * ch], preferred_element_type=F32)
        halo[c] = carry[c]
        g_buf[c] = gate
        carry[c] = gate[tm - FFN_HALO:, :]
        v_buf[c] = jnp.dot(xb16, wup_ref[:, D_FF + c * ch:D_FF + (c + 1) * ch],
                                      preferred_element_type=F32)

    def shifted(g, h, back):
        head_row = lax.broadcasted_iota(jnp.int32, h.shape, 0)
        head = jnp.where(head_row < back, pltpu.roll(h, back, axis=0),
                         pltpu.roll(g, back, axis=0)[:FFN_HALO])
        return jnp.concatenate([head, pltpu.roll(g, back, axis=0)[FFN_HALO:]], axis=0)

    n_ch = D_FF // ch
    acc = None
    up_proj(0)
    for c in range(n_ch):
        cols = slice(c * ch, (c + 1) * ch)
        if c + 1 < n_ch:
            up_proj(c + 1)
        g, h = g_buf[c], halo[c]
        conv = dw_b_ref[:, cols] + g * dw_w_ref[FFN_CONV_KERNEL - 1:FFN_CONV_KERNEL, cols]
        for k in range(FFN_CONV_KERNEL - 1):
            conv = conv + shifted(g, h, FFN_CONV_KERNEL - 1 - k) * dw_w_ref[k:k + 1, cols]
        act_buf[c] = (_gelu_tanh(conv) * v_buf[c]).astype(BF16)
        part = jnp.dot(act_buf[c], wd_ref[cols, :], preferred_element_type=F32)
        acc = part if acc is None else acc + part

    gate_p = jax.nn.sigmoid(
        jnp.dot(xb16, wpg_ref[...], preferred_element_type=F32) + bpg_ref[...])
    ple = gate_p * jnp.dot(p_ref[...].astype(BF16), wproj_ref[...],
                           preferred_element_type=F32)
    y = DEEPNORM_ALPHA * xb + acc + ple
    o_ref[...] = _layer_norm(y, ln_g_ref[...], ln_b_ref[...])


def _ffn(x, p, layer, w_up, dw_w, dw_b, w_down, w_proj, w_pgate, b_pgate, ln_g, ln_b):
    bsz, s, d = x.shape
    tm = FFN_TM
    return pl.pallas_call(
        _ffn_kernel,
        grid=(bsz, s // tm),
        in_specs=[
            pl.BlockSpec((None, tm, d), lambda b, i: (b, i, 0)),
            pl.BlockSpec((None, None, tm, PLE_DIM), lambda b, i: (layer, b, i, 0)),
            _const_spec((d, 2 * D_FF)),
            _const_spec((FFN_CONV_KERNEL, D_FF)),
            _const_spec((1, D_FF)),
            _const_spec((D_FF, d)),
            _const_spec((PLE_DIM, d)),
            _const_spec((d, d)),
            _const_spec((1, d)),
            _const_spec((1, d)),
            _const_spec((1, d)),
        ],
        out_specs=pl.BlockSpec((None, tm, d), lambda b, i: (b, i, 0)),
        out_shape=jax.ShapeDtypeStruct(x.shape, F32),
        scratch_shapes=[
            pltpu.VMEM((D_FF // FFN_CH, FFN_HALO, FFN_CH), F32),
            pltpu.VMEM((D_FF // FFN_CH, FFN_HALO, FFN_CH), F32),
            pltpu.VMEM((D_FF // FFN_CH, tm, FFN_CH), F32),
            pltpu.VMEM((D_FF // FFN_CH, tm, FFN_CH), F32),
            pltpu.VMEM((D_FF // FFN_CH, tm, FFN_CH), BF16),
        ],
        compiler_params=_params(("arbitrary", "arbitrary")),
        name="convffn",
    )(x, p, w_up, dw_w, dw_b, w_down, w_proj, w_pgate, b_pgate, ln_g, ln_b)


_NT = (((1,), (1,)), ((), ()))
_TN = (((0,), (0,)), ((), ()))


def _qkv_kernel(x_ref, wqt_ref, wk_ref, wvt_ref, qt_ref, k_ref, vt_ref):
    xb16 = x_ref[...].astype(BF16)
    qt_ref[...] = lax.dot_general(wqt_ref[...], xb16, _NT,
                                  preferred_element_type=F32).astype(BF16)
    k_ref[...] = jnp.dot(xb16, wk_ref[...], preferred_element_type=F32).astype(BF16)
    vt_ref[...] = lax.dot_general(wvt_ref[...], xb16, _NT,
                                  preferred_element_type=F32).astype(BF16)


def _qkv(x, wqt, wk, wvt):
    bsz, s, d = x.shape
    tm = QKV_TM
    row_spec = pl.BlockSpec((None, tm, d), lambda b, i: (b, i, 0))
    col_spec = pl.BlockSpec((None, d, tm), lambda b, i: (b, 0, i))
    return pl.pallas_call(
        _qkv_kernel,
        grid=(bsz, s // tm),
        in_specs=[row_spec, _const_spec((d, d)), _const_spec((d, d)), _const_spec((d, d))],
        out_specs=[col_spec, row_spec, col_spec],
        out_shape=[jax.ShapeDtypeStruct((bsz, d, s), BF16),
                   jax.ShapeDtypeStruct((bsz, s, d), BF16),
                   jax.ShapeDtypeStruct((bsz, d, s), BF16)],
        compiler_params=_params(("arbitrary", "arbitrary")),
        name="qkv",
    )(x, wqt, wk, wvt)


def _attn_kernel(qt_ref, k_ref, vt_ref, bias_ref, ot_ref, *s_bufs):
    s_len = k_ref.shape[0]
    tq = ATT_TQ
    n_kt, n_qt = ATT_NK // LANES, tq // LANES
    head_of_row = lax.broadcasted_iota(jnp.int32, (LANES, tq), 0) // HEAD_DIM

    def window_start(q0, nk):
        k0 = q0 + tq - nk
        return k0 if isinstance(k0, int) else pl.multiple_of(k0, tq)

    def scores(q0, nk, h):
        kwin = k_ref[pl.ds(window_start(q0, nk), nk), :]
        qblk = qt_ref[:, pl.ds(q0, tq)]
        qh = jnp.where(head_of_row == h, qblk, jnp.zeros_like(qblk))
        first_kt = n_kt - nk // LANES
        bias = jnp.concatenate([
            jnp.concatenate([
                bias_ref[h, :, (qt - kt + n_kt - 1) * LANES:(qt - kt + n_kt) * LANES]
                for qt in range(n_qt)], axis=1)
            for kt in range(first_kt, n_kt)], axis=0)
        return jnp.dot(kwin, qh, preferred_element_type=F32) + bias

    def softmax_pv(q0, nk, h, sc):
        m = jnp.max(sc, axis=0, keepdims=True)
        pr = jnp.exp2(sc - m).astype(BF16)
        vwin = jnp.concatenate(
            [vt_ref[h * HEAD_DIM:(h + 1) * HEAD_DIM, pl.ds(window_start(q0, nk), nk)],
             jnp.ones((BF16_SUBLANES, nk), BF16)], axis=0)
        o = jnp.dot(vwin, pr, preferred_element_type=F32)
        ot_ref[h * HEAD_DIM:(h + 1) * HEAD_DIM, pl.ds(q0, tq)] = (
            o[:HEAD_DIM] / o[HEAD_DIM:HEAD_DIM + 1]).astype(BF16)

    items = [(b, h) for b in range(s_len // tq) for h in range(HEADS_PER_STEP)]

    def window(b):
        return min((b + 1) * tq, ATT_NK)

    def fill(n):
        b, h = items[n]
        s_bufs[n % len(s_bufs)][0:window(b), :] = scores(b * tq, window(b), h)

    for n in range(min(ATT_AHEAD, len(items))):
        fill(n)
    for n, (b, h) in enumerate(items):
        if n + ATT_AHEAD < len(items):
            fill(n + ATT_AHEAD)
        softmax_pv(b * tq, window(b), h, s_bufs[n % len(s_bufs)][0:window(b), :])


def _attention(qt, k, vt, bias_t):
    bsz, s, d = k.shape
    n_groups = d // LANES
    lane_spec = pl.BlockSpec((None, LANES, s), lambda g, b: (b, g, 0))
    return pl.pallas_call(
        _attn_kernel,
        grid=(n_groups, bsz),
        in_specs=[
            lane_spec,
            pl.BlockSpec((None, s, LANES), lambda g, b: (b, 0, g)),
            lane_spec,
            pl.BlockSpec((HEADS_PER_STEP,) + bias_t.shape[1:], lambda g, b: (g, 0, 0)),
        ],
        out_specs=lane_spec,
        out_shape=jax.ShapeDtypeStruct((bsz, d, s), BF16),
        scratch_shapes=[pltpu.VMEM((ATT_NK, ATT_TQ), F32)] * (ATT_AHEAD + 1),
        compiler_params=_params(("arbitrary", "arbitrary")),
        name="band_attn",
    )(qt, k, vt, bias_t)


def _oproj_kernel(ot_ref, x_ref, wo_ref, ln_g_ref, ln_b_ref, o_ref):
    mix = lax.dot_general(ot_ref[...], wo_ref[...], _TN, preferred_element_type=F32)
    o_ref[...] = _layer_norm(DEEPNORM_ALPHA * x_ref[...] + mix, ln_g_ref[...], ln_b_ref[...])


def _oproj(ot, x, wo, ln_g, ln_b):
    bsz, s, d = x.shape
    tm = OPROJ_TM
    row_spec = pl.BlockSpec((None, tm, d), lambda b, i: (b, i, 0))
    return pl.pallas_call(
        _oproj_kernel,
        grid=(bsz, s // tm),
        in_specs=[pl.BlockSpec((None, d, tm), lambda b, i: (b, 0, i)), row_spec,
                  _const_spec((d, d)), _const_spec((1, d)), _const_spec((1, d))],
        out_specs=row_spec,
        out_shape=jax.ShapeDtypeStruct(x.shape, F32),
        compiler_params=_params(("arbitrary", "arbitrary")),
        name="attn_oproj",
    )(ot, x, wo, ln_g, ln_b)


def _attn_bias_table(rel_bias):
    n_heads = rel_bias.shape[0]
    n_m = (ATT_NK + ATT_TQ) // LANES - 1
    length = (n_m + 1) * LANES - 1
    dist = jnp.arange(length) - (LANES - 1) - (ATT_NK - LANES - ATT_HIST)
    per_dist = rel_bias[:, jnp.clip(dist, -MAX_REL_DIST, MAX_REL_DIST) + MAX_REL_DIST]
    ext = jnp.pad(per_dist.astype(F32), ((0, 0), (0, 1)))
    skew = jnp.tile(ext, (1, LANES))[:, :LANES * length].reshape(n_heads, LANES, length)
    table = skew[:, :, LANES - 1:]
    a = jnp.arange(LANES)[:, None]
    c = jnp.arange(n_m * LANES)[None, :]
    m, b = c // LANES, c % LANES
    chunk_gap = ((m - (ATT_NK // LANES - 1)) * (LANES // CHUNK) + LEFT_CHUNKS
                 + b // CHUNK - a // CHUNK)
    visible = (chunk_gap >= 0) & (chunk_gap <= LEFT_CHUNKS)
    return jnp.where(visible[None], table * LOG2_E, NEG_INF)


def _row(v):
    return v.reshape(1, -1)


def kernel(x, p, mix_w_in, pool_w, pool_scale, conv_dw_w, conv_dw_b, conv_ln_g, conv_ln_b,
           mix_w_out, attn_w_qkv, attn_rel_bias, attn_w_o, ln_mix_g, ln_mix_b, ffn_w_up,
           ffn_dw_w, ffn_dw_b, ffn_w_down, ple_w_proj, ple_w_gate, ple_b_gate, ln_ffn_g,
           ln_ffn_b):
    def ffn(xi, i):
        return _ffn(xi, p, i, ffn_w_up[i].astype(BF16), ffn_dw_w[i], _row(ffn_dw_b[i]),
                    ffn_w_down[i].astype(BF16), ple_w_proj[i].astype(BF16),
                    ple_w_gate[i].astype(BF16), _row(ple_b_gate[i]),
                    _row(ln_ffn_g[i]), _row(ln_ffn_b[i]))

    x = _mixer(x, mix_w_in[0].astype(BF16), pool_w[0].astype(BF16), _row(pool_scale[0]),
               conv_dw_w[0], _row(conv_dw_b[0]), _row(conv_ln_g[0]), _row(conv_ln_b[0]),
               mix_w_out[0].astype(BF16), _row(ln_mix_g[0]), _row(ln_mix_b[0]))
    x = ffn(x, 0)

    w_qkv = attn_w_qkv[0]
    wqt = (w_qkv[:, :D_MODEL].T * (HEAD_DIM ** -0.5 * LOG2_E)).astype(BF16)
    wk = w_qkv[:, D_MODEL:2 * D_MODEL].astype(BF16)
    wvt = w_qkv[:, 2 * D_MODEL:].T.astype(BF16)
    qt, k, vt = _qkv(x, wqt, wk, wvt)
    ot = _attention(qt, k, vt, _attn_bias_table(attn_rel_bias[0]))
    x = _oproj(ot, x, attn_w_o[0].astype(BF16), _row(ln_mix_g[1]), _row(ln_mix_b[1]))
    x = ffn(x, 1)
    return x
```

```python
import functools

import jax
import jax.numpy as jnp
from jax import lax
from jax.experimental import pallas as pl
from jax.experimental.pallas import tpu as pltpu

F32 = jnp.float32
BF16 = jnp.bfloat16

D_MODEL = 1024
DEPTH = 2
CHUNK = 64
PLE_DIM = 256
D_POOL = 512
D_CONV = 512
POOL_WINDOWS = (2, 4, 8, 16)
POOL_GROUP = 128
CONV_KERNEL = 31
HEAD_DIM = 64
N_HEADS = 16
LEFT_CHUNKS = 8
MAX_REL_DIST = 256
D_FF = 2816
FFN_CONV_KERNEL = 3
DEEPNORM_ALPHA = (2 * DEPTH) ** 0.25
LN_EPS = 1e-5
NEG_INF = -1e30
LOG2_E = 1.4426950408889634
GELU_C1 = -2.0 * LOG2_E * 0.7978845608028654
GELU_C3 = GELU_C1 * 0.044715

SUBLANES = 8
BF16_SUBLANES = 16
LANES = 128
VMEM_LIMIT_BYTES = 56 * 1024 * 1024

MIX_TM = 512
MIX_RC = 32
POOL_HALO = 16
CONV_HALO = 32
FFN_TM = 256
FFN_SUB = 2
FFN_CH = 256
FFN_HALO = SUBLANES
QKV_TM = 512
ATT_TQ = 256
ATT_HIST = LEFT_CHUNKS * CHUNK
ATT_NK = ATT_HIST + ATT_TQ
HEADS_PER_STEP = LANES // HEAD_DIM
ATT_AHEAD = 2
OPROJ_TM = 512
OPROJ_SUB_TM = 256


def _layer_norm(v, g, b):
    mu = jnp.mean(v, axis=-1, keepdims=True)
    c = v - mu
    var = jnp.mean(c * c, axis=-1, keepdims=True)
    return c * lax.rsqrt(var + LN_EPS) * g + b


def _gelu_tanh(v):
    u2 = v * (GELU_C1 + GELU_C3 * (v * v))
    return v / (1.0 + jnp.exp2(u2))


def _params(semantics):
    return pltpu.CompilerParams(dimension_semantics=semantics,
                                vmem_limit_bytes=VMEM_LIMIT_BYTES)


def _const_spec(shape):
    zeros = (0,) * len(shape)
    return pl.BlockSpec(shape, lambda *_: zeros, pipeline_mode=pl.Buffered(1))


def _mixer_kernel(x_ref, w_in_ref, pool_w_ref, pool_scale_ref, dw_w_ref, dw_b_ref,
                  cn_g_ref, cn_b_ref, w_out_ref, ln_g_ref, ln_b_ref, o_ref,
                  a_ext, g_ext, g_ph, conv_out, cat):
    i = pl.program_id(1)
    tm = x_ref.shape[0]

    @pl.when(i == 0)
    def _():
        a_ext[0:POOL_HALO, :] = jnp.zeros((POOL_HALO, D_POOL), F32)
        g_ext[0:CONV_HALO, :] = jnp.zeros((CONV_HALO, D_CONV), F32)

    @pl.when(i > 0)
    def _():
        a_ext[0:POOL_HALO, :] = a_ext[tm:tm + POOL_HALO, :]
        g_ext[0:CONV_HALO, :] = g_ext[tm:tm + CONV_HALO, :]

    xb = x_ref[...]
    u = jnp.dot(xb.astype(BF16), w_in_ref[...], preferred_element_type=F32)
    a_ext[POOL_HALO:, :] = u[:, :D_POOL]
    g_ext[CONV_HALO:, :] = (u[:, D_POOL:D_POOL + D_CONV]
                            * jax.nn.sigmoid(u[:, D_POOL + D_CONV:]))

    row = lax.broadcasted_iota(jnp.int32, (tm, POOL_GROUP), 0)
    pos = (row + (i * tm + 1)).astype(F32)
    for g, w in enumerate(POOL_WINDOWS):
        cols = slice(g * POOL_GROUP, (g + 1) * POOL_GROUP)
        ext = a_ext[:, cols]
        cur = ext[POOL_HALO:]
        s, span = ext, 1
        while span < w:
            s = s + pltpu.roll(s, span, axis=0)
            span *= 2
        d = s[POOL_HALO:] / jnp.minimum(pos, float(w)) - cur
        ya = jnp.dot(d.astype(BF16), pool_w_ref[g], preferred_element_type=F32)
        cat[:, cols] = (ya * pool_scale_ref[:, cols]).astype(BF16)

    g_all = g_ext[...]
    n_ext = g_all.shape[0]
    for p in range(1, SUBLANES):
        g_ph[p - 1, :, :] = pltpu.roll(g_all, n_ext - p, axis=0)[:g_ph.shape[1]]

    def conv_chunk(c, carry):
        r0 = pl.multiple_of(c * MIX_RC, MIX_RC)
        acc = None
        for k in range(CONV_KERNEL):
            off = CONV_HALO - (CONV_KERNEL - 1) + k
            q, p = divmod(off, SUBLANES)
            if p == 0:
                t = g_ext[pl.ds(r0 + SUBLANES * q, MIX_RC), :]
            else:
                t = g_ph[p - 1, pl.ds(r0 + SUBLANES * q, MIX_RC), :]
            t = t * dw_w_ref[k:k + 1, :]
            acc = t if acc is None else acc + t
        conv_out[pl.ds(r0, MIX_RC), :] = acc
        return carry

    lax.fori_loop(0, tm // MIX_RC, conv_chunk, 0)
    h = _layer_norm(conv_out[...] + dw_b_ref[...], cn_g_ref[...], cn_b_ref[...])
    cat[:, D_POOL:] = (h * jax.nn.sigmoid(h)).astype(BF16)

    mix = jnp.dot(cat[...], w_out_ref[...], preferred_element_type=F32)
    o_ref[...] = _layer_norm(DEEPNORM_ALPHA * xb + mix, ln_g_ref[...], ln_b_ref[...])


def _mixer(x, w_in, pool_w, pool_scale, dw_w, dw_b, cn_g, cn_b, w_out, ln_g, ln_b):
    bsz, s, d = x.shape
    tm = MIX_TM
    d_in = w_in.shape[1]
    return pl.pallas_call(
        _mixer_kernel,
        grid=(bsz, s // tm),
        in_specs=[
            pl.BlockSpec((None, tm, d), lambda b, i: (b, i, 0)),
            _const_spec((d, d_in)),
            _const_spec(pool_w.shape),
            _const_spec((1, D_POOL)),
            _const_spec((CONV_KERNEL, D_CONV)),
            _const_spec((1, D_CONV)),
            _const_spec((1, D_CONV)),
            _const_spec((1, D_CONV)),
            _const_spec((d, d)),
            _const_spec((1, d)),
            _const_spec((1, d)),
        ],
        out_specs=pl.BlockSpec((None, tm, d), lambda b, i: (b, i, 0)),
        out_shape=jax.ShapeDtypeStruct(x.shape, F32),
        scratch_shapes=[
            pltpu.VMEM((tm + POOL_HALO, D_POOL), F32),
            pltpu.VMEM((tm + CONV_HALO, D_CONV), F32),
            pltpu.VMEM((SUBLANES - 1, tm + CONV_HALO - SUBLANES, D_CONV), F32),
            pltpu.VMEM((tm, D_CONV), F32),
            pltpu.VMEM((tm, d), BF16),
        ],
        compiler_params=_params(("arbitrary", "arbitrary")),
        name="mixer0",
    )(x, w_in, pool_w, pool_scale, dw_w, dw_b, cn_g, cn_b, w_out, ln_g, ln_b)


def _ffn_kernel(x_ref, p_ref, wup_ref, dw_w_ref, dw_b_ref, wd_ref,
                wproj_ref, wpg_ref, bpg_ref, ln_g_ref, ln_b_ref, o_ref, carry):
    i = pl.program_id(1)
    tm, ch = FFN_TM, FFN_CH
    n_ch = D_FF // ch

    @pl.when(i == 0)
    def _():
        carry[...] = jnp.zeros(carry.shape, F32)

    def shifted(g, h, back):
        head_row = lax.broadcasted_iota(jnp.int32, h.shape, 0)
        head = jnp.where(head_row < back, pltpu.roll(h, back, axis=0),
                         pltpu.roll(g, back, axis=0)[:FFN_HALO])
        return jnp.concatenate([head, pltpu.roll(g, back, axis=0)[FFN_HALO:]], axis=0)

    for sub in range(x_ref.shape[0] // tm):
        rows = slice(sub * tm, (sub + 1) * tm)
        xb = x_ref[rows, :]
        xb16 = xb.astype(BF16)

        def up_proj(c):
            gate = jnp.dot(xb16, wup_ref[:, c * ch:(c + 1) * ch], preferred_element_type=F32)
            h = carry[c]
            carry[c] = gate[tm - FFN_HALO:, :]
            val = jnp.dot(xb16, wup_ref[:, D_FF + c * ch:D_FF + (c + 1) * ch],
                          preferred_element_type=F32)
            return gate, h, val

        acc = None
        nxt = up_proj(0)
        for c in range(n_ch):
            cols = slice(c * ch, (c + 1) * ch)
            g, h, val = nxt
            if c + 1 < n_ch:
                nxt = up_proj(c + 1)
            conv = dw_b_ref[:, cols] + g * dw_w_ref[FFN_CONV_KERNEL - 1:FFN_CONV_KERNEL, cols]
            for k in range(FFN_CONV_KERNEL - 1):
                conv = conv + shifted(g, h, FFN_CONV_KERNEL - 1 - k) * dw_w_ref[k:k + 1, cols]
            act = (_gelu_tanh(conv) * val).astype(BF16)
            part = jnp.dot(act, wd_ref[cols, :], preferred_element_type=F32)
            acc = part if acc is None else acc + part

        gate_p = jax.nn.sigmoid(
            jnp.dot(xb16, wpg_ref[...], preferred_element_type=F32) + bpg_ref[...])
        ple = gate_p * jnp.dot(p_ref[rows, :].astype(BF16), wproj_ref[...],
                               preferred_element_type=F32)
        y = DEEPNORM_ALPHA * xb + acc + ple
        o_ref[rows, :] = _layer_norm(y, ln_g_ref[...], ln_b_ref[...])


def _ffn(x, p, layer, w_up, dw_w, dw_b, w_down, w_proj, w_pgate, b_pgate, ln_g, ln_b):
    bsz, s, d = x.shape
    tb = FFN_TM * FFN_SUB
    return pl.pallas_call(
        _ffn_kernel,
        grid=(bsz, s // tb),
        in_specs=[
            pl.BlockSpec((None, tb, d), lambda b, i: (b, i, 0)),
            pl.BlockSpec((None, None, tb, PLE_DIM), lambda b, i: (layer, b, i, 0)),
            _const_spec((d, 2 * D_FF)),
            _const_spec((FFN_CONV_KERNEL, D_FF)),
            _const_spec((1, D_FF)),
            _const_spec((D_FF, d)),
            _const_spec((PLE_DIM, d)),
            _const_spec((d, d)),
            _const_spec((1, d)),
            _const_spec((1, d)),
            _const_spec((1, d)),
        ],
        out_specs=pl.BlockSpec((None, tb, d), lambda b, i: (b, i, 0)),
        out_shape=jax.ShapeDtypeStruct(x.shape, F32),
        scratch_shapes=[pltpu.VMEM((D_FF // FFN_CH, FFN_HALO, FFN_CH), F32)],
        compiler_params=_params(("arbitrary", "arbitrary")),
        name="convffn",
    )(x, p, w_up, dw_w, dw_b, w_down, w_proj, w_pgate, b_pgate, ln_g, ln_b)


_NT = (((1,), (1,)), ((), ()))
_TN = (((0,), (0,)), ((), ()))


def _qkv_kernel(x_ref, wqt_ref, wk_ref, wvt_ref, qt_ref, k_ref, vt_ref):
    xb16 = x_ref[...].astype(BF16)
    qt_ref[...] = lax.dot_general(wqt_ref[...], xb16, _NT,
                                  preferred_element_type=F32).astype(BF16)
    k_ref[...] = jnp.dot(xb16, wk_ref[...], preferred_element_type=F32).astype(BF16)
    vt_ref[...] = lax.dot_general(wvt_ref[...], xb16, _NT,
                                  preferred_element_type=F32).astype(BF16)


def _qkv(x, wqt, wk, wvt):
    bsz, s, d = x.shape
    tm = QKV_TM
    row_spec = pl.BlockSpec((None, tm, d), lambda b, i: (b, i, 0))
    col_spec = pl.BlockSpec((None, d, tm), lambda b, i: (b, 0, i))
    return pl.pallas_call(
        _qkv_kernel,
        grid=(bsz, s // tm),
        in_specs=[row_spec, _const_spec((d, d)), _const_spec((d, d)), _const_spec((d, d))],
        out_specs=[col_spec, row_spec, col_spec],
        out_shape=[jax.ShapeDtypeStruct((bsz, d, s), BF16),
                   jax.ShapeDtypeStruct((bsz, s, d), BF16),
                   jax.ShapeDtypeStruct((bsz, d, s), BF16)],
        compiler_params=_params(("arbitrary", "arbitrary")),
        name="qkv",
    )(x, wqt, wk, wvt)


def _attn_kernel(qt_ref, k_ref, vt_ref, bias_ref, ot_ref, *s_bufs):
    s_len = k_ref.shape[0]
    tq = ATT_TQ
    n_kt, n_qt = ATT_NK // LANES, tq // LANES
    head_of_row = lax.broadcasted_iota(jnp.int32, (LANES, tq), 0) // HEAD_DIM

    def window_start(q0, nk):
        k0 = q0 + tq - nk
        return k0 if isinstance(k0, int) else pl.multiple_of(k0, tq)

    def scores(q0, nk, h):
        kwin = k_ref[pl.ds(window_start(q0, nk), nk), :]
        qblk = qt_ref[:, pl.ds(q0, tq)]
        qh = jnp.where(head_of_row == h, qblk, jnp.zeros_like(qblk))
        first_kt = n_kt - nk // LANES
        bias = jnp.concatenate([
            jnp.concatenate([
                bias_ref[h, :, (qt - kt + n_kt - 1) * LANES:(qt - kt + n_kt) * LANES]
                for qt in range(n_qt)], axis=1)
            for kt in range(first_kt, n_kt)], axis=0)
        return jnp.dot(kwin, qh, preferred_element_type=F32) + bias

    def softmax_pv(q0, nk, h, sc):
        m = jnp.max(sc, axis=0, keepdims=True)
        pr = jnp.exp2(sc - m).astype(BF16)
        vwin = jnp.concatenate(
            [vt_ref[h * HEAD_DIM:(h + 1) * HEAD_DIM, pl.ds(window_start(q0, nk), nk)],
             jnp.ones((BF16_SUBLANES, nk), BF16)], axis=0)
        o = jnp.dot(vwin, pr, preferred_element_type=F32)
        ot_ref[h * HEAD_DIM:(h + 1) * HEAD_DIM, pl.ds(q0, tq)] = (
            o[:HEAD_DIM] / o[HEAD_DIM:HEAD_DIM + 1]).astype(BF16)

    items = [(b, h) for b in range(s_len // tq) for h in range(HEADS_PER_STEP)]

    def window(b):
        return min((b + 1) * tq, ATT_NK)

    def fill(n):
        b, h = items[n]
        s_bufs[n % len(s_bufs)][0:window(b), :] = scores(b * tq, window(b), h)

    for n in range(min(ATT_AHEAD, len(items))):
        fill(n)
    for n, (b, h) in enumerate(items):
        if n + ATT_AHEAD < len(items):
            fill(n + ATT_AHEAD)
        softmax_pv(b * tq, window(b), h, s_bufs[n % len(s_bufs)][0:window(b), :])


def _attention(qt, k, vt, bias_t):
    bsz, s, d = k.shape
    n_groups = d // LANES
    lane_spec = pl.BlockSpec((None, LANES, s), lambda g, b: (b, g, 0))
    return pl.pallas_call(
        _attn_kernel,
        grid=(n_groups, bsz),
        in_specs=[
            lane_spec,
            pl.BlockSpec((None, s, LANES), lambda g, b: (b, 0, g)),
            lane_spec,
            pl.BlockSpec((HEADS_PER_STEP,) + bias_t.shape[1:], lambda g, b: (g, 0, 0)),
        ],
        out_specs=lane_spec,
        out_shape=jax.ShapeDtypeStruct((bsz, d, s), BF16),
        scratch_shapes=[pltpu.VMEM((ATT_NK, ATT_TQ), F32)] * (ATT_AHEAD + 1),
        compiler_params=_params(("arbitrary", "arbitrary")),
        name="band_attn",
    )(qt, k, vt, bias_t)


def _oproj_kernel(ot_ref, x_ref, wo_ref, ln_g_ref, ln_b_ref, o_ref):
    for sub in range(x_ref.shape[0] // OPROJ_SUB_TM):
        rows = slice(sub * OPROJ_SUB_TM, (sub + 1) * OPROJ_SUB_TM)
        mix = lax.dot_general(ot_ref[:, rows], wo_ref[...], _TN, preferred_element_type=F32)
        o_ref[rows, :] = _layer_norm(DEEPNORM_ALPHA * x_ref[rows, :] + mix,
                                     ln_g_ref[...], ln_b_ref[...])


def _oproj(ot, x, wo, ln_g, ln_b):
    bsz, s, d = x.shape
    tm = OPROJ_TM
    row_spec = pl.BlockSpec((None, tm, d), lambda b, i: (b, i, 0))
    return pl.pallas_call(
        _oproj_kernel,
        grid=(bsz, s // tm),
        in_specs=[pl.BlockSpec((None, d, tm), lambda b, i: (b, 0, i)), row_spec,
                  _const_spec((d, d)), _const_spec((1, d)), _const_spec((1, d))],
        out_specs=row_spec,
        out_shape=jax.ShapeDtypeStruct(x.shape, F32),
        compiler_params=_params(("arbitrary", "arbitrary")),
        name="attn_oproj",
    )(ot, x, wo, ln_g, ln_b)


def _attn_bias_table(rel_bias):
    n_heads = rel_bias.shape[0]
    n_m = (ATT_NK + ATT_TQ) // LANES - 1
    length = (n_m + 1) * LANES - 1
    dist = jnp.arange(length) - (LANES - 1) - (ATT_NK - LANES - ATT_HIST)
    per_dist = rel_bias[:, jnp.clip(dist, -MAX_REL_DIST, MAX_REL_DIST) + MAX_REL_DIST]
    ext = jnp.pad(per_dist.astype(F32), ((0, 0), (0, 1)))
    skew = jnp.tile(ext, (1, LANES))[:, :LANES * length].reshape(n_heads, LANES, length)
    table = skew[:, :, LANES - 1:]
    a = jnp.arange(LANES)[:, None]
    c = jnp.arange(n_m * LANES)[None, :]
    m, b = c // LANES, c % LANES
    chunk_gap = ((m - (ATT_NK // LANES - 1)) * (LANES // CHUNK) + LEFT_CHUNKS
                 + b // CHUNK - a // CHUNK)
    visible = (chunk_gap >= 0) & (chunk_gap <= LEFT_CHUNKS)
    return jnp.where(visible[None], table * LOG2_E, NEG_INF)


def _row(v):
    return v.reshape(1, -1)


def kernel(x, p, mix_w_in, pool_w, pool_scale, conv_dw_w, conv_dw_b, conv_ln_g, conv_ln_b,
           mix_w_out, attn_w_qkv, attn_rel_bias, attn_w_o, ln_mix_g, ln_mix_b, ffn_w_up,
           ffn_dw_w, ffn_dw_b, ffn_w_down, ple_w_proj, ple_w_gate, ple_b_gate, ln_ffn_g,
           ln_ffn_b):
    def ffn(xi, i):
        return _ffn(xi, p, i, ffn_w_up[i].astype(BF16), ffn_dw_w[i], _row(ffn_dw_b[i]),
                    ffn_w_down[i].astype(BF16), ple_w_proj[i].astype(BF16),
                    ple_w_gate[i].astype(BF16), _row(ple_b_gate[i]),
                    _row(ln_ffn_g[i]), _row(ln_ffn_b[i]))

    x = _mixer(x, mix_w_in[0].astype(BF16), pool_w[0].astype(BF16), _row(pool_scale[0]),
               conv_dw_w[0], _row(conv_dw_b[0]), _row(conv_ln_g[0]), _row(conv_ln_b[0]),
               mix_w_out[0].astype(BF16), _row(ln_mix_g[0]), _row(ln_mix_b[0]))
    x = ffn(x, 0)

    w_qkv = attn_w_qkv[0]
    wqt = (w_qkv[:, :D_MODEL].T * (HEAD_DIM ** -0.5 * LOG2_E)).astype(BF16)
    wk = w_qkv[:, D_MODEL:2 * D_MODEL].astype(BF16)
    wvt = w_qkv[:, 2 * D_MODEL:].T.astype(BF16)
    qt, k, vt = _qkv(x, wqt, wk, wvt)
    ot = _attention(qt, k, vt, _attn_bias_table(attn_rel_bias[0]))
    x = _oproj(ot, x, attn_w_o[0].astype(BF16), _row(ln_mix_g[1]), _row(ln_mix_b[1]))
    x = ffn(x, 1)
    return x
```

```python
import functools

import jax
import jax.numpy as jnp
from jax import lax
from jax.experimental import pallas as pl
from jax.experimental.pallas import tpu as pltpu

F32 = jnp.float32
BF16 = jnp.bfloat16

D_MODEL = 1024
DEPTH = 2
CHUNK = 64
PLE_DIM = 256
D_POOL = 512
D_CONV = 512
POOL_WINDOWS = (2, 4, 8, 16)
POOL_GROUP = 128
CONV_KERNEL = 31
HEAD_DIM = 64
N_HEADS = 16
LEFT_CHUNKS = 8
MAX_REL_DIST = 256
D_FF = 2816
FFN_CONV_KERNEL = 3
DEEPNORM_ALPHA = (2 * DEPTH) ** 0.25
LN_EPS = 1e-5
NEG_INF = -1e30
LOG2_E = 1.4426950408889634
GELU_C1 = -2.0 * LOG2_E * 0.7978845608028654
GELU_C3 = GELU_C1 * 0.044715

SUBLANES = 8
BF16_SUBLANES = 16
LANES = 128
VMEM_LIMIT_BYTES = 56 * 1024 * 1024

MIX_TM = 512
MIX_RC = 32
POOL_HALO = 16
CONV_HALO = 32
FFN_TM = 256
FFN_SUB = 2
FFN_CH = 256
FFN_HALO = SUBLANES
QKV_TM = 512
ATT_TQ = 256
ATT_HIST = LEFT_CHUNKS * CHUNK
ATT_NK = ATT_HIST + ATT_TQ
HEADS_PER_STEP = LANES // HEAD_DIM
ATT_AHEAD = 2

_NT = (((1,), (1,)), ((), ()))
_TN = (((0,), (0,)), ((), ()))


def _layer_norm(v, g, b):
    mu = jnp.mean(v, axis=-1, keepdims=True)
    c = v - mu
    var = jnp.mean(c * c, axis=-1, keepdims=True)
    return c * lax.rsqrt(var + LN_EPS) * g + b


def _gelu_tanh(v):
    u2 = v * (GELU_C1 + GELU_C3 * (v * v))
    return v / (1.0 + jnp.exp2(u2))


def _params(semantics):
    return pltpu.CompilerParams(dimension_semantics=semantics,
                                vmem_limit_bytes=VMEM_LIMIT_BYTES)


def _const_spec(shape):
    zeros = (0,) * len(shape)
    return pl.BlockSpec(shape, lambda *_: zeros, pipeline_mode=pl.Buffered(1))


def _mixer_kernel(x_ref, w_in_ref, pool_w_ref, pool_scale_ref, dw_w_ref, dw_b_ref,
                  cn_g_ref, cn_b_ref, w_out_ref, ln_g_ref, ln_b_ref, o_ref,
                  a_ext, g_ext, g_ph, conv_out, cat):
    i = pl.program_id(1)
    tm = x_ref.shape[0]

    @pl.when(i == 0)
    def _():
        a_ext[0:POOL_HALO, :] = jnp.zeros((POOL_HALO, D_POOL), F32)
        g_ext[0:CONV_HALO, :] = jnp.zeros((CONV_HALO, D_CONV), F32)

    @pl.when(i > 0)
    def _():
        a_ext[0:POOL_HALO, :] = a_ext[tm:tm + POOL_HALO, :]
        g_ext[0:CONV_HALO, :] = g_ext[tm:tm + CONV_HALO, :]

    xb = x_ref[...]
    u = jnp.dot(xb.astype(BF16), w_in_ref[...], preferred_element_type=F32)
    a_ext[POOL_HALO:, :] = u[:, :D_POOL]
    g_ext[CONV_HALO:, :] = (u[:, D_POOL:D_POOL + D_CONV]
                            * jax.nn.sigmoid(u[:, D_POOL + D_CONV:]))

    row = lax.broadcasted_iota(jnp.int32, (tm, POOL_GROUP), 0)
    pos = (row + (i * tm + 1)).astype(F32)
    for g, w in enumerate(POOL_WINDOWS):
        cols = slice(g * POOL_GROUP, (g + 1) * POOL_GROUP)
        ext = a_ext[:, cols]
        cur = ext[POOL_HALO:]
        s, span = ext, 1
        while span < w:
            s = s + pltpu.roll(s, span, axis=0)
            span *= 2
        d = s[POOL_HALO:] / jnp.minimum(pos, float(w)) - cur
        ya = jnp.dot(d.astype(BF16), pool_w_ref[g], preferred_element_type=F32)
        cat[:, cols] = (ya * pool_scale_ref[:, cols]).astype(BF16)

    g_all = g_ext[...]
    n_ext = g_all.shape[0]
    for p in range(1, SUBLANES):
        g_ph[p - 1, :, :] = pltpu.roll(g_all, n_ext - p, axis=0)[:g_ph.shape[1]]

    def conv_chunk(c, carry):
        r0 = pl.multiple_of(c * MIX_RC, MIX_RC)
        acc = None
        for k in range(CONV_KERNEL):
            off = CONV_HALO - (CONV_KERNEL - 1) + k
            q, p = divmod(off, SUBLANES)
            if p == 0:
                t = g_ext[pl.ds(r0 + SUBLANES * q, MIX_RC), :]
            else:
                t = g_ph[p - 1, pl.ds(r0 + SUBLANES * q, MIX_RC), :]
            t = t * dw_w_ref[k:k + 1, :]
            acc = t if acc is None else acc + t
        conv_out[pl.ds(r0, MIX_RC), :] = acc
        return carry

    lax.fori_loop(0, tm // MIX_RC, conv_chunk, 0)
    h = _layer_norm(conv_out[...] + dw_b_ref[...], cn_g_ref[...], cn_b_ref[...])
    cat[:, D_POOL:] = (h * jax.nn.sigmoid(h)).astype(BF16)

    mix = jnp.dot(cat[...], w_out_ref[...], preferred_element_type=F32)
    o_ref[...] = _layer_norm(DEEPNORM_ALPHA * xb + mix, ln_g_ref[...], ln_b_ref[...])


def _mixer(x, w_in, pool_w, pool_scale, dw_w, dw_b, cn_g, cn_b, w_out, ln_g, ln_b):
    bsz, s, d = x.shape
    tm = MIX_TM
    d_in = w_in.shape[1]
    return pl.pallas_call(
        _mixer_kernel,
        grid=(bsz, s // tm),
        in_specs=[
            pl.BlockSpec((None, tm, d), lambda b, i: (b, i, 0)),
            _const_spec((d, d_in)),
            _const_spec(pool_w.shape),
            _const_spec((1, D_POOL)),
            _const_spec((CONV_KERNEL, D_CONV)),
            _const_spec((1, D_CONV)),
            _const_spec((1, D_CONV)),
            _const_spec((1, D_CONV)),
            _const_spec((d, d)),
            _const_spec((1, d)),
            _const_spec((1, d)),
        ],
        out_specs=pl.BlockSpec((None, tm, d), lambda b, i: (b, i, 0)),
        out_shape=jax.ShapeDtypeStruct(x.shape, F32),
        scratch_shapes=[
            pltpu.VMEM((tm + POOL_HALO, D_POOL), F32),
            pltpu.VMEM((tm + CONV_HALO, D_CONV), F32),
            pltpu.VMEM((SUBLANES - 1, tm + CONV_HALO - SUBLANES, D_CONV), F32),
            pltpu.VMEM((tm, D_CONV), F32),
            pltpu.VMEM((tm, d), BF16),
        ],
        compiler_params=_params(("arbitrary", "arbitrary")),
        name="mixer0",
    )(x, w_in, pool_w, pool_scale, dw_w, dw_b, cn_g, cn_b, w_out, ln_g, ln_b)


def _ffn_kernel(*refs, with_attn_out):
    if with_attn_out:
        ht_ref, wo_ref, lnm_g_ref, lnm_b_ref = refs[:4]
        refs = refs[4:]
    (x_ref, p_ref, wup_ref, dw_w_ref, dw_b_ref, wd_ref, wproj_ref, wpg_ref, bpg_ref,
     ln_g_ref, ln_b_ref, o_ref, carry) = refs
    i = pl.program_id(1)
    tm, ch = FFN_TM, FFN_CH
    n_ch = D_FF // ch

    @pl.when(i == 0)
    def _():
        carry[...] = jnp.zeros(carry.shape, F32)

    def shifted(g, h, back):
        head_row = lax.broadcasted_iota(jnp.int32, h.shape, 0)
        head = jnp.where(head_row < back, pltpu.roll(h, back, axis=0),
                         pltpu.roll(g, back, axis=0)[:FFN_HALO])
        return jnp.concatenate([head, pltpu.roll(g, back, axis=0)[FFN_HALO:]], axis=0)

    for sub in range(x_ref.shape[0] // tm):
        rows = slice(sub * tm, (sub + 1) * tm)
        xb = x_ref[rows, :]
        if with_attn_out:
            mix = lax.dot_general(ht_ref[:, rows], wo_ref[...], _TN, preferred_element_type=F32)
            xb = _layer_norm(DEEPNORM_ALPHA * xb + mix, lnm_g_ref[...], lnm_b_ref[...])
        xb16 = xb.astype(BF16)

        def up_proj(c):
            gate = jnp.dot(xb16, wup_ref[:, c * ch:(c + 1) * ch], preferred_element_type=F32)
            h = carry[c]
            carry[c] = gate[tm - FFN_HALO:, :]
            val = jnp.dot(xb16, wup_ref[:, D_FF + c * ch:D_FF + (c + 1) * ch],
                          preferred_element_type=F32)
            return gate, h, val

        acc = None
        nxt = up_proj(0)
        for c in range(n_ch):
            cols = slice(c * ch, (c + 1) * ch)
            g, h, val = nxt
            if c + 1 < n_ch:
                nxt = up_proj(c + 1)
            conv = dw_b_ref[:, cols] + g * dw_w_ref[FFN_CONV_KERNEL - 1:FFN_CONV_KERNEL, cols]
            for k in range(FFN_CONV_KERNEL - 1):
                conv = conv + shifted(g, h, FFN_CONV_KERNEL - 1 - k) * dw_w_ref[k:k + 1, cols]
            act = (_gelu_tanh(conv) * val).astype(BF16)
            part = jnp.dot(act, wd_ref[cols, :], preferred_element_type=F32)
            acc = part if acc is None else acc + part

        gate_p = jax.nn.sigmoid(
            jnp.dot(xb16, wpg_ref[...], preferred_element_type=F32) + bpg_ref[...])
        ple = gate_p * jnp.dot(p_ref[rows, :].astype(BF16), wproj_ref[...],
                               preferred_element_type=F32)
        y = DEEPNORM_ALPHA * xb + acc + ple
        o_ref[rows, :] = _layer_norm(y, ln_g_ref[...], ln_b_ref[...])


def _ffn(x, p, layer, w_up, dw_w, dw_b, w_down, w_proj, w_pgate, b_pgate, ln_g, ln_b,
         attn_out=None):
    bsz, s, d = x.shape
    tb = FFN_TM * FFN_SUB
    in_specs = [
        pl.BlockSpec((None, tb, d), lambda b, i: (b, i, 0)),
        pl.BlockSpec((None, None, tb, PLE_DIM), lambda b, i: (layer, b, i, 0)),
        _const_spec((d, 2 * D_FF)),
        _const_spec((FFN_CONV_KERNEL, D_FF)),
        _const_spec((1, D_FF)),
        _const_spec((D_FF, d)),
        _const_spec((PLE_DIM, d)),
        _const_spec((d, d)),
        _const_spec((1, d)),
        _const_spec((1, d)),
        _const_spec((1, d)),
    ]
    args = (x, p, w_up, dw_w, dw_b, w_down, w_proj, w_pgate, b_pgate, ln_g, ln_b)
    if attn_out is not None:
        in_specs = [pl.BlockSpec((None, d, tb), lambda b, i: (b, 0, i)), _const_spec((d, d)),
                    _const_spec((1, d)), _const_spec((1, d))] + in_specs
        args = tuple(attn_out) + args
    return pl.pallas_call(
        functools.partial(_ffn_kernel, with_attn_out=attn_out is not None),
        grid=(bsz, s // tb),
        in_specs=in_specs,
        out_specs=pl.BlockSpec((None, tb, d), lambda b, i: (b, i, 0)),
        out_shape=jax.ShapeDtypeStruct(x.shape, F32),
        scratch_shapes=[pltpu.VMEM((D_FF // FFN_CH, FFN_HALO, FFN_CH), F32)],
        compiler_params=_params(("arbitrary", "arbitrary")),
        name="convffn",
    )(*args)


def _qkv_kernel(x_ref, wqt_ref, wk_ref, wvt_ref, qt_ref, k_ref, vt_ref):
    xb16 = x_ref[...].astype(BF16)
    qt_ref[...] = lax.dot_general(wqt_ref[...], xb16, _NT,
                                  preferred_element_type=F32).astype(BF16)
    k = jnp.dot(xb16, wk_ref[...], preferred_element_type=F32).astype(BF16)
    for g in range(k_ref.shape[0]):
        k_ref[g] = k[:, g * LANES:(g + 1) * LANES]
    vt_ref[...] = lax.dot_general(wvt_ref[...], xb16, _NT,
                                  preferred_element_type=F32).astype(BF16)


def _qkv(x, wqt, wk, wvt):
    bsz, s, d = x.shape
    tm = QKV_TM
    row_spec = pl.BlockSpec((None, tm, d), lambda b, i: (b, i, 0))
    col_spec = pl.BlockSpec((None, d, tm), lambda b, i: (b, 0, i))
    return pl.pallas_call(
        _qkv_kernel,
        grid=(bsz, s // tm),
        in_specs=[row_spec, _const_spec((d, d)), _const_spec((d, d)), _const_spec((d, d))],
        out_specs=[col_spec,
                   pl.BlockSpec((None, d // LANES, tm, LANES), lambda b, i: (b, 0, i, 0)),
                   col_spec],
        out_shape=[jax.ShapeDtypeStruct((bsz, d, s), BF16),
                   jax.ShapeDtypeStruct((bsz, d // LANES, s, LANES), BF16),
                   jax.ShapeDtypeStruct((bsz, d, s), BF16)],
        compiler_params=_params(("arbitrary", "arbitrary")),
        name="qkv",
    )(x, wqt, wk, wvt)


def _attn_kernel(qt_ref, k_ref, vt_ref, bias_ref, ot_ref, *s_bufs):
    s_len = k_ref.shape[0]
    tq = ATT_TQ
    n_kt, n_qt = ATT_NK // LANES, tq // LANES
    head_of_row = lax.broadcasted_iota(jnp.int32, (LANES, tq), 0) // HEAD_DIM

    def window_start(q0, nk):
        k0 = q0 + tq - nk
        return k0 if isinstance(k0, int) else pl.multiple_of(k0, tq)

    def scores(q0, nk, h):
        kwin = k_ref[pl.ds(window_start(q0, nk), nk), :]
        qblk = qt_ref[:, pl.ds(q0, tq)]
        qh = jnp.where(head_of_row == h, qblk, jnp.zeros_like(qblk))
        first_kt = n_kt - nk // LANES
        bias = jnp.concatenate([
            jnp.concatenate([
                bias_ref[h, :, (qt - kt + n_kt - 1) * LANES:(qt - kt + n_kt) * LANES]
                for qt in range(n_qt)], axis=1)
            for kt in range(first_kt, n_kt)], axis=0)
        return jnp.dot(kwin, qh, preferred_element_type=F32) + bias

    def softmax_pv(q0, nk, h, sc):
        m = jnp.max(sc, axis=0, keepdims=True)
        pr = jnp.exp2(sc - m).astype(BF16)
        vwin = jnp.concatenate(
            [vt_ref[h * HEAD_DIM:(h + 1) * HEAD_DIM, pl.ds(window_start(q0, nk), nk)],
             jnp.ones((BF16_SUBLANES, nk), BF16)], axis=0)
        o = jnp.dot(vwin, pr, preferred_element_type=F32)
        ot_ref[h * HEAD_DIM:(h + 1) * HEAD_DIM, pl.ds(q0, tq)] = (
            o[:HEAD_DIM] / o[HEAD_DIM:HEAD_DIM + 1]).astype(BF16)

    items = [(b, h) for b in range(s_len // tq) for h in range(HEADS_PER_STEP)]

    def window(b):
        return min((b + 1) * tq, ATT_NK)

    def fill(n):
        b, h = items[n]
        s_bufs[n % len(s_bufs)][0:window(b), :] = scores(b * tq, window(b), h)

    for n in range(min(ATT_AHEAD, len(items))):
        fill(n)
    for n, (b, h) in enumerate(items):
        if n + ATT_AHEAD < len(items):
            fill(n + ATT_AHEAD)
        softmax_pv(b * tq, window(b), h, s_bufs[n % len(s_bufs)][0:window(b), :])


def _attention(qt, k, vt, bias_t):
    bsz, d, s = qt.shape
    n_groups = d // LANES
    lane_spec = pl.BlockSpec((None, LANES, s), lambda g, b: (b, g, 0))
    return pl.pallas_call(
        _attn_kernel,
        grid=(n_groups, bsz),
        in_specs=[
            lane_spec,
            pl.BlockSpec((None, None, s, LANES), lambda g, b: (b, g, 0, 0)),
            lane_spec,
            pl.BlockSpec((HEADS_PER_STEP,) + bias_t.shape[1:], lambda g, b: (g, 0, 0)),
        ],
        out_specs=lane_spec,
        out_shape=jax.ShapeDtypeStruct((bsz, d, s), BF16),
        scratch_shapes=[pltpu.VMEM((ATT_NK, ATT_TQ), F32)] * (ATT_AHEAD + 1),
        compiler_params=_params(("arbitrary", "arbitrary")),
        name="band_attn",
    )(qt, k, vt, bias_t)


def _attn_bias_table(rel_bias):
    n_heads = rel_bias.shape[0]
    n_m = (ATT_NK + ATT_TQ) // LANES - 1
    length = (n_m + 1) * LANES - 1
    dist = jnp.arange(length) - (LANES - 1) - (ATT_NK - LANES - ATT_HIST)
    per_dist = rel_bias[:, jnp.clip(dist, -MAX_REL_DIST, MAX_REL_DIST) + MAX_REL_DIST]
    ext = jnp.pad(per_dist.astype(F32), ((0, 0), (0, 1)))
    skew = jnp.tile(ext, (1, LANES))[:, :LANES * length].reshape(n_heads, LANES, length)
    table = skew[:, :, LANES - 1:]
    a = jnp.arange(LANES)[:, None]
    c = jnp.arange(n_m * LANES)[None, :]
    m, b = c // LANES, c % LANES
    chunk_gap = ((m - (ATT_NK // LANES - 1)) * (LANES // CHUNK) + LEFT_CHUNKS
                 + b // CHUNK - a // CHUNK)
    visible = (chunk_gap >= 0) & (chunk_gap <= LEFT_CHUNKS)
    return jnp.where(visible[None], table * LOG2_E, NEG_INF)


def _row(v):
    return v.reshape(1, -1)


def kernel(x, p, mix_w_in, pool_w, pool_scale, conv_dw_w, conv_dw_b, conv_ln_g, conv_ln_b,
           mix_w_out, attn_w_qkv, attn_rel_bias, attn_w_o, ln_mix_g, ln_mix_b, ffn_w_up,
           ffn_dw_w, ffn_dw_b, ffn_w_down, ple_w_proj, ple_w_gate, ple_b_gate, ln_ffn_g,
           ln_ffn_b):
    def ffn(xi, i, attn_out=None):
        return _ffn(xi, p, i, ffn_w_up[i].astype(BF16), ffn_dw_w[i], _row(ffn_dw_b[i]),
                    ffn_w_down[i].astype(BF16), ple_w_proj[i].astype(BF16),
                    ple_w_gate[i].astype(BF16), _row(ple_b_gate[i]),
                    _row(ln_ffn_g[i]), _row(ln_ffn_b[i]), attn_out=attn_out)

    x = _mixer(x, mix_w_in[0].astype(BF16), pool_w[0].astype(BF16), _row(pool_scale[0]),
               conv_dw_w[0], _row(conv_dw_b[0]), _row(conv_ln_g[0]), _row(conv_ln_b[0]),
               mix_w_out[0].astype(BF16), _row(ln_mix_g[0]), _row(ln_mix_b[0]))
    x = ffn(x, 0)

    w_qkv = attn_w_qkv[0]
    wqt = (w_qkv[:, :D_MODEL].T * (HEAD_DIM ** -0.5 * LOG2_E)).astype(BF16)
    wk = w_qkv[:, D_MODEL:2 * D_MODEL].astype(BF16)
    wvt = w_qkv[:, 2 * D_MODEL:].T.astype(BF16)
    qt, k, vt = _qkv(x, wqt, wk, wvt)
    heads_t = _attention(qt, k, vt, _attn_bias_table(attn_rel_bias[0]))
    x = ffn(x, 1, attn_out=(heads_t, attn_w_o[0].astype(BF16),
                            _row(ln_mix_g[1]), _row(ln_mix_b[1])))
    return x
```

```python
import functools

import jax
import jax.numpy as jnp
from jax import lax
from jax.experimental import pallas as pl
from jax.experimental.pallas import tpu as pltpu

F32 = jnp.float32
BF16 = jnp.bfloat16

D_MODEL = 1024
DEPTH = 2
CHUNK = 64
PLE_DIM = 256
D_POOL = 512
D_CONV = 512
POOL_WINDOWS = (2, 4, 8, 16)
POOL_GROUP = 128
CONV_KERNEL = 31
HEAD_DIM = 64
N_HEADS = 16
LEFT_CHUNKS = 8
MAX_REL_DIST = 256
D_FF = 2816
FFN_CONV_KERNEL = 3
DEEPNORM_ALPHA = (2 * DEPTH) ** 0.25
LN_EPS = 1e-5
NEG_INF = -1e30
LOG2_E = 1.4426950408889634
GELU_C1 = -2.0 * LOG2_E * 0.7978845608028654
GELU_C3 = GELU_C1 * 0.044715

SUBLANES = 8
BF16_SUBLANES = 16
LANES = 128
VMEM_LIMIT_BYTES = 56 * 1024 * 1024

MIX_TM = 512
MIX_RC = 64
POOL_HALO = 16
CONV_HALO = 32
FFN_TM = 256
FFN_SUB = 2
FFN_CH = 256
FFN_HALO = SUBLANES
QKV_TM = 512
ATT_TQ = 256
ATT_HIST = LEFT_CHUNKS * CHUNK
ATT_NK = ATT_HIST + ATT_TQ
HEADS_PER_STEP = LANES // HEAD_DIM
ATT_AHEAD = 2
CAST_ROWS = 256

_NT = (((1,), (1,)), ((), ()))
_TN = (((0,), (0,)), ((), ()))


def _layer_norm(v, g, b):
    mu = jnp.mean(v, axis=-1, keepdims=True)
    c = v - mu
    var = jnp.mean(c * c, axis=-1, keepdims=True)
    return c * lax.rsqrt(var + LN_EPS) * g + b


def _gelu_tanh(v):
    u2 = v * (GELU_C1 + GELU_C3 * (v * v))
    return v / (1.0 + jnp.exp2(u2))


def _params(semantics):
    return pltpu.CompilerParams(dimension_semantics=semantics,
                                vmem_limit_bytes=VMEM_LIMIT_BYTES)


def _const_spec(shape):
    zeros = (0,) * len(shape)
    return pl.BlockSpec(shape, lambda *_: zeros, pipeline_mode=pl.Buffered(1))


def _mixer_kernel(x_ref, w_in_ref, pool_w_ref, pool_scale_ref, dw_w_ref, dw_b_ref,
                  cn_g_ref, cn_b_ref, w_out_ref, ln_g_ref, ln_b_ref, o_ref,
                  a_ext, g_ext, g_ph, conv_out, cat):
    i = pl.program_id(1)
    tm = x_ref.shape[0]

    @pl.when(i == 0)
    def _():
        a_ext[0:POOL_HALO, :] = jnp.zeros((POOL_HALO, D_POOL), F32)
        g_ext[0:CONV_HALO, :] = jnp.zeros((CONV_HALO, D_CONV), F32)

    @pl.when(i > 0)
    def _():
        a_ext[0:POOL_HALO, :] = a_ext[tm:tm + POOL_HALO, :]
        g_ext[0:CONV_HALO, :] = g_ext[tm:tm + CONV_HALO, :]

    xb = x_ref[...]
    u = jnp.dot(xb.astype(BF16), w_in_ref[...], preferred_element_type=F32)
    a_ext[POOL_HALO:, :] = u[:, :D_POOL]
    g_ext[CONV_HALO:, :] = (u[:, D_POOL:D_POOL + D_CONV]
                            * jax.nn.sigmoid(u[:, D_POOL + D_CONV:]))

    row = lax.broadcasted_iota(jnp.int32, (tm, POOL_GROUP), 0)
    pos = (row + (i * tm + 1)).astype(F32)
    for g, w in enumerate(POOL_WINDOWS):
        cols = slice(g * POOL_GROUP, (g + 1) * POOL_GROUP)
        ext = a_ext[:, cols]
        cur = ext[POOL_HALO:]
        s, span = ext, 1
        while span < w:
            s = s + pltpu.roll(s, span, axis=0)
            span *= 2
        d = s[POOL_HALO:] / jnp.minimum(pos, float(w)) - cur
        ya = jnp.dot(d.astype(BF16), pool_w_ref[g], preferred_element_type=F32)
        cat[:, cols] = (ya * pool_scale_ref[:, cols]).astype(BF16)

    g_all = g_ext[...]
    n_ext = g_all.shape[0]
    for p in range(1, SUBLANES):
        g_ph[p - 1, :, :] = pltpu.roll(g_all, n_ext - p, axis=0)[:g_ph.shape[1]]

    def conv_chunk(c, carry):
        r0 = pl.multiple_of(c * MIX_RC, MIX_RC)
        acc = None
        for k in range(CONV_KERNEL):
            off = CONV_HALO - (CONV_KERNEL - 1) + k
            q, p = divmod(off, SUBLANES)
            if p == 0:
                t = g_ext[pl.ds(r0 + SUBLANES * q, MIX_RC), :]
            else:
                t = g_ph[p - 1, pl.ds(r0 + SUBLANES * q, MIX_RC), :]
            t = t * dw_w_ref[k:k + 1, :]
            acc = t if acc is None else acc + t
        conv_out[pl.ds(r0, MIX_RC), :] = acc
        return carry

    lax.fori_loop(0, tm // MIX_RC, conv_chunk, 0)
    h = _layer_norm(conv_out[...] + dw_b_ref[...], cn_g_ref[...], cn_b_ref[...])
    cat[:, D_POOL:] = (h * jax.nn.sigmoid(h)).astype(BF16)

    mix = jnp.dot(cat[...], w_out_ref[...], preferred_element_type=F32)
    o_ref[...] = _layer_norm(DEEPNORM_ALPHA * xb + mix, ln_g_ref[...], ln_b_ref[...])


def _mixer(x, w_in, pool_w, pool_scale, dw_w, dw_b, cn_g, cn_b, w_out, ln_g, ln_b):
    bsz, s, d = x.shape
    tm = MIX_TM
    d_in = w_in.shape[1]
    return pl.pallas_call(
        _mixer_kernel,
        grid=(bsz, s // tm),
        in_specs=[
            pl.BlockSpec((None, tm, d), lambda b, i: (b, i, 0)),
            _const_spec((d, d_in)),
            _const_spec(pool_w.shape),
            _const_spec((1, D_POOL)),
            _const_spec((CONV_KERNEL, D_CONV)),
            _const_spec((1, D_CONV)),
            _const_spec((1, D_CONV)),
            _const_spec((1, D_CONV)),
            _const_spec((d, d)),
            _const_spec((1, d)),
            _const_spec((1, d)),
        ],
        out_specs=pl.BlockSpec((None, tm, d), lambda b, i: (b, i, 0)),
        out_shape=jax.ShapeDtypeStruct(x.shape, F32),
        scratch_shapes=[
            pltpu.VMEM((tm + POOL_HALO, D_POOL), F32),
            pltpu.VMEM((tm + CONV_HALO, D_CONV), F32),
            pltpu.VMEM((SUBLANES - 1, tm + CONV_HALO - SUBLANES, D_CONV), F32),
            pltpu.VMEM((tm, D_CONV), F32),
            pltpu.VMEM((tm, d), BF16),
        ],
        compiler_params=_params(("arbitrary", "arbitrary")),
        name="mixer0",
    )(x, w_in, pool_w, pool_scale, dw_w, dw_b, cn_g, cn_b, w_out, ln_g, ln_b)


def _ffn_kernel(*refs, with_attn_out):
    if with_attn_out:
        ht_ref, wo_ref, lnm_g_ref, lnm_b_ref = refs[:4]
        refs = refs[4:]
    (x_ref, p_ref, wup_ref, dw_w_ref, dw_b_ref, wd_ref, wproj_ref, wpg_ref, bpg_ref,
     ln_g_ref, ln_b_ref, o_ref, carry) = refs
    i = pl.program_id(1)
    tm, ch = FFN_TM, FFN_CH
    n_ch = D_FF // ch

    @pl.when(i == 0)
    def _():
        carry[...] = jnp.zeros(carry.shape, F32)

    def shifted(g, h, back):
        head_row = lax.broadcasted_iota(jnp.int32, h.shape, 0)
        head = jnp.where(head_row < back, pltpu.roll(h, back, axis=0),
                         pltpu.roll(g, back, axis=0)[:FFN_HALO])
        return jnp.concatenate([head, pltpu.roll(g, back, axis=0)[FFN_HALO:]], axis=0)

    for sub in range(x_ref.shape[0] // tm):
        rows = slice(sub * tm, (sub + 1) * tm)
        xb = x_ref[rows, :]
        if with_attn_out:
            mix = lax.dot_general(ht_ref[:, rows], wo_ref[...], _TN, preferred_element_type=F32)
            xb = _layer_norm(DEEPNORM_ALPHA * xb + mix, lnm_g_ref[...], lnm_b_ref[...])
        xb16 = xb.astype(BF16)

        def up_proj(c):
            gate = jnp.dot(xb16, wup_ref[:, c * ch:(c + 1) * ch], preferred_element_type=F32)
            h = carry[c]
            carry[c] = gate[tm - FFN_HALO:, :]
            val = jnp.dot(xb16, wup_ref[:, D_FF + c * ch:D_FF + (c + 1) * ch],
                          preferred_element_type=F32)
            return gate, h, val

        acc = None
        nxt = up_proj(0)
        for c in range(n_ch):
            cols = slice(c * ch, (c + 1) * ch)
            g, h, val = nxt
            if c + 1 < n_ch:
                nxt = up_proj(c + 1)
            conv = dw_b_ref[:, cols] + g * dw_w_ref[FFN_CONV_KERNEL - 1:FFN_CONV_KERNEL, cols]
            for k in range(FFN_CONV_KERNEL - 1):
                conv = conv + shifted(g, h, FFN_CONV_KERNEL - 1 - k) * dw_w_ref[k:k + 1, cols]
            act = (_gelu_tanh(conv) * val).astype(BF16)
            part = jnp.dot(act, wd_ref[cols, :], preferred_element_type=F32)
            acc = part if acc is None else acc + part

        gate_p = jax.nn.sigmoid(
            jnp.dot(xb16, wpg_ref[...], preferred_element_type=F32) + bpg_ref[...])
        ple = gate_p * jnp.dot(p_ref[rows, :].astype(BF16), wproj_ref[...],
                               preferred_element_type=F32)
        y = DEEPNORM_ALPHA * xb + acc + ple
        o_ref[rows, :] = _layer_norm(y, ln_g_ref[...], ln_b_ref[...])


def _ffn(x, p, layer, w_up, dw_w, dw_b, w_down, w_proj, w_pgate, b_pgate, ln_g, ln_b,
         attn_out=None):
    bsz, s, d = x.shape
    tb = FFN_TM * FFN_SUB
    in_specs = [
        pl.BlockSpec((None, tb, d), lambda b, i: (b, i, 0)),
        pl.BlockSpec((None, None, tb, PLE_DIM), lambda b, i: (layer, b, i, 0)),
        _const_spec((d, 2 * D_FF)),
        _const_spec((FFN_CONV_KERNEL, D_FF)),
        _const_spec((1, D_FF)),
        _const_spec((D_FF, d)),
        _const_spec((PLE_DIM, d)),
        _const_spec((d, d)),
        _const_spec((1, d)),
        _const_spec((1, d)),
        _const_spec((1, d)),
    ]
    args = (x, p, w_up, dw_w, dw_b, w_down, w_proj, w_pgate, b_pgate, ln_g, ln_b)
    if attn_out is not None:
        in_specs = [pl.BlockSpec((None, d, tb), lambda b, i: (b, 0, i)), _const_spec((d, d)),
                    _const_spec((1, d)), _const_spec((1, d))] + in_specs
        args = tuple(attn_out) + args
    return pl.pallas_call(
        functools.partial(_ffn_kernel, with_attn_out=attn_out is not None),
        grid=(bsz, s // tb),
        in_specs=in_specs,
        out_specs=pl.BlockSpec((None, tb, d), lambda b, i: (b, i, 0)),
        out_shape=jax.ShapeDtypeStruct(x.shape, F32),
        scratch_shapes=[pltpu.VMEM((D_FF // FFN_CH, FFN_HALO, FFN_CH), F32)],
        compiler_params=_params(("arbitrary", "arbitrary")),
        name="convffn",
    )(*args)


def _qkv_kernel(x_ref, wqt_ref, wk_ref, wvt_ref, qt_ref, k_ref, vt_ref):
    xb16 = x_ref[...].astype(BF16)
    qt_ref[...] = lax.dot_general(wqt_ref[...], xb16, _NT,
                                  preferred_element_type=F32).astype(BF16)
    k_ref[...] = jnp.dot(xb16, wk_ref[...], preferred_element_type=F32).astype(BF16)
    vt_ref[...] = lax.dot_general(wvt_ref[...], xb16, _NT,
                                  preferred_element_type=F32).astype(BF16)


def _qkv(x, wqt, wk, wvt):
    bsz, s, d = x.shape
    tm = QKV_TM
    row_spec = pl.BlockSpec((None, tm, d), lambda b, i: (b, i, 0))
    col_spec = pl.BlockSpec((None, d, tm), lambda b, i: (b, 0, i))
    return pl.pallas_call(
        _qkv_kernel,
        grid=(bsz, s // tm),
        in_specs=[row_spec, _const_spec((d, d)), _const_spec((d, d)), _const_spec((d, d))],
        out_specs=[col_spec, row_spec, col_spec],
        out_shape=[jax.ShapeDtypeStruct((bsz, d, s), BF16),
                   jax.ShapeDtypeStruct((bsz, s, d), BF16),
                   jax.ShapeDtypeStruct((bsz, d, s), BF16)],
        compiler_params=_params(("arbitrary", "arbitrary")),
        name="qkv",
    )(x, wqt, wk, wvt)


def _attn_kernel(qt_ref, k_ref, vt_ref, bias_ref, ot_ref, *s_bufs):
    s_len = k_ref.shape[0]
    tq = ATT_TQ
    n_kt, n_qt = ATT_NK // LANES, tq // LANES
    head_of_row = lax.broadcasted_iota(jnp.int32, (LANES, tq), 0) // HEAD_DIM

    def window_start(q0, nk):
        k0 = q0 + tq - nk
        return k0 if isinstance(k0, int) else pl.multiple_of(k0, tq)

    def scores(q0, nk, h):
        kwin = k_ref[pl.ds(window_start(q0, nk), nk), :]
        qblk = qt_ref[:, pl.ds(q0, tq)]
        qh = jnp.where(head_of_row == h, qblk, jnp.zeros_like(qblk))
        first_kt = n_kt - nk // LANES
        bias = jnp.concatenate([
            jnp.concatenate([
                bias_ref[h, :, (qt - kt + n_kt - 1) * LANES:(qt - kt + n_kt) * LANES]
                for qt in range(n_qt)], axis=1)
            for kt in range(first_kt, n_kt)], axis=0)
        return jnp.dot(kwin, qh, preferred_element_type=F32) + bias

    def softmax_pv(q0, nk, h, sc):
        m = jnp.max(sc, axis=0, keepdims=True)
        pr = jnp.exp2(sc - m).astype(BF16)
        vwin = jnp.concatenate(
            [vt_ref[h * HEAD_DIM:(h + 1) * HEAD_DIM, pl.ds(window_start(q0, nk), nk)],
             jnp.ones((BF16_SUBLANES, nk), BF16)], axis=0)
        o = jnp.dot(vwin, pr, preferred_element_type=F32)
        ot_ref[h * HEAD_DIM:(h + 1) * HEAD_DIM, pl.ds(q0, tq)] = (
            o[:HEAD_DIM] / o[HEAD_DIM:HEAD_DIM + 1]).astype(BF16)

    items = [(b, h) for b in range(s_len // tq) for h in range(HEADS_PER_STEP)]

    def window(b):
        return min((b + 1) * tq, ATT_NK)

    def fill(n):
        b, h = items[n]
        s_bufs[n % len(s_bufs)][0:window(b), :] = scores(b * tq, window(b), h)

    for n in range(min(ATT_AHEAD, len(items))):
        fill(n)
    for n, (b, h) in enumerate(items):
        if n + ATT_AHEAD < len(items):
            fill(n + ATT_AHEAD)
        softmax_pv(b * tq, window(b), h, s_bufs[n % len(s_bufs)][0:window(b), :])


def _attention(qt, k, vt, bias_t):
    bsz, d, s = qt.shape
    n_groups = d // LANES
    lane_spec = pl.BlockSpec((None, LANES, s), lambda g, b: (b, g, 0))
    return pl.pallas_call(
        _attn_kernel,
        grid=(n_groups, bsz),
        in_specs=[
            lane_spec,
            pl.BlockSpec((None, s, LANES), lambda g, b: (b, 0, g)),
            lane_spec,
            pl.BlockSpec((HEADS_PER_STEP,) + bias_t.shape[1:], lambda g, b: (g, 0, 0)),
        ],
        out_specs=lane_spec,
        out_shape=jax.ShapeDtypeStruct((bsz, d, s), BF16),
        scratch_shapes=[pltpu.VMEM((ATT_NK, ATT_TQ), F32)] * (ATT_AHEAD + 1),
        compiler_params=_params(("arbitrary", "arbitrary")),
        name="band_attn",
    )(qt, k, vt, bias_t)


def _attn_bias_table(rel_bias):
    n_heads = rel_bias.shape[0]
    n_m = (ATT_NK + ATT_TQ) // LANES - 1
    length = (n_m + 1) * LANES - 1
    dist = jnp.arange(length) - (LANES - 1) - (ATT_NK - LANES - ATT_HIST)
    per_dist = rel_bias[:, jnp.clip(dist, -MAX_REL_DIST, MAX_REL_DIST) + MAX_REL_DIST]
    ext = jnp.pad(per_dist.astype(F32), ((0, 0), (0, 1)))
    skew = jnp.tile(ext, (1, LANES))[:, :LANES * length].reshape(n_heads, LANES, length)
    table = skew[:, :, LANES - 1:]
    a = jnp.arange(LANES)[:, None]
    c = jnp.arange(n_m * LANES)[None, :]
    m, b = c // LANES, c % LANES
    chunk_gap = ((m - (ATT_NK // LANES - 1)) * (LANES // CHUNK) + LEFT_CHUNKS
                 + b // CHUNK - a // CHUNK)
    visible = (chunk_gap >= 0) & (chunk_gap <= LEFT_CHUNKS)
    return jnp.where(visible[None], table * LOG2_E, NEG_INF)


def _row(v):
    return v.reshape(1, -1)


def _cast_kernel(w_ref, o_ref):
    o_ref[...] = w_ref[...].astype(BF16)


def _layer_bf16(w, layer):
    _, r, c = w.shape
    tr = CAST_ROWS
    return pl.pallas_call(
        _cast_kernel,
        grid=(r // tr,),
        in_specs=[pl.BlockSpec((None, tr, c), lambda i: (layer, i, 0))],
        out_specs=pl.BlockSpec((tr, c), lambda i: (i, 0)),
        out_shape=jax.ShapeDtypeStruct((r, c), BF16),
        compiler_params=_params(("arbitrary",)),
        name="cast_bf16",
    )(w)


def kernel(x, p, mix_w_in, pool_w, pool_scale, conv_dw_w, conv_dw_b, conv_ln_g, conv_ln_b,
           mix_w_out, attn_w_qkv, attn_rel_bias, attn_w_o, ln_mix_g, ln_mix_b, ffn_w_up,
           ffn_dw_w, ffn_dw_b, ffn_w_down, ple_w_proj, ple_w_gate, ple_b_gate, ln_ffn_g,
           ln_ffn_b):
    def ffn(xi, i, attn_out=None):
        return _ffn(xi, p, i, _layer_bf16(ffn_w_up, i), ffn_dw_w[i], _row(ffn_dw_b[i]),
                    _layer_bf16(ffn_w_down, i), ple_w_proj[i].astype(BF16),
                    _layer_bf16(ple_w_gate, i), _row(ple_b_gate[i]),
                    _row(ln_ffn_g[i]), _row(ln_ffn_b[i]), attn_out=attn_out)

    x = _mixer(x, mix_w_in[0].astype(BF16), pool_w[0].astype(BF16), _row(pool_scale[0]),
               conv_dw_w[0], _row(conv_dw_b[0]), _row(conv_ln_g[0]), _row(conv_ln_b[0]),
               mix_w_out[0].astype(BF16), _row(ln_mix_g[0]), _row(ln_mix_b[0]))
    x = ffn(x, 0)

    w_qkv = attn_w_qkv[0]
    wqt = (w_qkv[:, :D_MODEL].T * (HEAD_DIM ** -0.5 * LOG2_E)).astype(BF16)
    wk = w_qkv[:, D_MODEL:2 * D_MODEL].astype(BF16)
    wvt = w_qkv[:, 2 * D_MODEL:].T.astype(BF16)
    qt, k, vt = _qkv(x, wqt, wk, wvt)
    heads_t = _attention(qt, k, vt, _attn_bias_table(attn_rel_bias[0]))
    x = ffn(x, 1, attn_out=(heads_t, attn_w_o[0].astype(BF16),
                            _row(ln_mix_g[1]), _row(ln_mix_b[1])))
    return x
```

```python
import functools

import jax
import jax.numpy as jnp
from jax import lax
from jax.experimental import pallas as pl
from jax.experimental.pallas import tpu as pltpu

F32 = jnp.float32
BF16 = jnp.bfloat16

D_MODEL = 1024
DEPTH = 2
CHUNK = 64
PLE_DIM = 256
D_POOL = 512
D_CONV = 512
POOL_WINDOWS = (2, 4, 8, 16)
POOL_GROUP = 128
CONV_KERNEL = 31
HEAD_DIM = 64
N_HEADS = 16
LEFT_CHUNKS = 8
MAX_REL_DIST = 256
D_FF = 2816
FFN_CONV_KERNEL = 3
DEEPNORM_ALPHA = (2 * DEPTH) ** 0.25
LN_EPS = 1e-5
NEG_INF = -1e30
LOG2_E = 1.4426950408889634
GELU_C1 = -2.0 * LOG2_E * 0.7978845608028654
GELU_C3 = GELU_C1 * 0.044715

SUBLANES = 8
BF16_SUBLANES = 16
LANES = 128
VMEM_LIMIT_BYTES = 56 * 1024 * 1024

MIX_TM = 512
MIX_RC = 64
POOL_HALO = 16
CONV_HALO = 32
FFN_TM = 256
FFN_SUB = 2
FFN_CH = 256
FFN_HALO = SUBLANES
QKV_TM = 512
ATT_TQ = 256
ATT_HIST = LEFT_CHUNKS * CHUNK
ATT_NK = ATT_HIST + ATT_TQ
HEADS_PER_STEP = LANES // HEAD_DIM
ATT_AHEAD = 2
CAST_ROWS = 256

_NT = (((1,), (1,)), ((), ()))
_TN = (((0,), (0,)), ((), ()))


def _layer_norm(v, g, b):
    mu = jnp.mean(v, axis=-1, keepdims=True)
    c = v - mu
    var = jnp.mean(c * c, axis=-1, keepdims=True)
    return c * lax.rsqrt(var + LN_EPS) * g + b


def _gelu_tanh(v):
    u2 = v * (GELU_C1 + GELU_C3 * (v * v))
    return v / (1.0 + jnp.exp2(u2))


def _params(semantics):
    return pltpu.CompilerParams(dimension_semantics=semantics,
                                vmem_limit_bytes=VMEM_LIMIT_BYTES)


def _const_spec(shape):
    zeros = (0,) * len(shape)
    return pl.BlockSpec(shape, lambda *_: zeros, pipeline_mode=pl.Buffered(1))


def _mixer_kernel(x_ref, w_in_ref, pool_w_ref, pool_scale_ref, dw_w_ref, dw_b_ref,
                  cn_g_ref, cn_b_ref, w_out_ref, ln_g_ref, ln_b_ref, o_ref,
                  a_ext, g_ext, g_ph, conv_out, cat):
    i = pl.program_id(1)
    tm = x_ref.shape[0]

    @pl.when(i == 0)
    def _():
        a_ext[0:POOL_HALO, :] = jnp.zeros((POOL_HALO, D_POOL), F32)
        g_ext[0:CONV_HALO, :] = jnp.zeros((CONV_HALO, D_CONV), F32)

    @pl.when(i > 0)
    def _():
        a_ext[0:POOL_HALO, :] = a_ext[tm:tm + POOL_HALO, :]
        g_ext[0:CONV_HALO, :] = g_ext[tm:tm + CONV_HALO, :]

    xb = x_ref[...]
    u = jnp.dot(xb.astype(BF16), w_in_ref[...], preferred_element_type=F32)
    a_ext[POOL_HALO:, :] = u[:, :D_POOL]
    g_ext[CONV_HALO:, :] = (u[:, D_POOL:D_POOL + D_CONV]
                            * jax.nn.sigmoid(u[:, D_POOL + D_CONV:]))

    row = lax.broadcasted_iota(jnp.int32, (tm, POOL_GROUP), 0)
    pos = (row + (i * tm + 1)).astype(F32)
    for g, w in enumerate(POOL_WINDOWS):
        cols = slice(g * POOL_GROUP, (g + 1) * POOL_GROUP)
        ext = a_ext[:, cols]
        cur = ext[POOL_HALO:]
        s, span = ext, 1
        while span < w:
            s = s + pltpu.roll(s, span, axis=0)
            span *= 2
        d = s[POOL_HALO:] / jnp.minimum(pos, float(w)) - cur
        ya = jnp.dot(d.astype(BF16), pool_w_ref[g], preferred_element_type=F32)
        cat[:, cols] = (ya * pool_scale_ref[:, cols]).astype(BF16)

    g_all = g_ext[...]
    n_ext = g_all.shape[0]
    for p in range(1, SUBLANES):
        g_ph[p - 1, :, :] = pltpu.roll(g_all, n_ext - p, axis=0)[:g_ph.shape[1]]

    def conv_chunk(c, carry):
        r0 = pl.multiple_of(c * MIX_RC, MIX_RC)
        acc = None
        for k in range(CONV_KERNEL):
            off = CONV_HALO - (CONV_KERNEL - 1) + k
            q, p = divmod(off, SUBLANES)
            if p == 0:
                t = g_ext[pl.ds(r0 + SUBLANES * q, MIX_RC), :]
            else:
                t = g_ph[p - 1, pl.ds(r0 + SUBLANES * q, MIX_RC), :]
            t = t * dw_w_ref[k:k + 1, :]
            acc = t if acc is None else acc + t
        conv_out[pl.ds(r0, MIX_RC), :] = acc
        return carry

    lax.fori_loop(0, tm // MIX_RC, conv_chunk, 0)
    h = _layer_norm(conv_out[...] + dw_b_ref[...], cn_g_ref[...], cn_b_ref[...])
    cat[:, D_POOL:] = (h * jax.nn.sigmoid(h)).astype(BF16)

    mix = jnp.dot(cat[...], w_out_ref[...], preferred_element_type=F32)
    o_ref[...] = _layer_norm(DEEPNORM_ALPHA * xb + mix, ln_g_ref[...], ln_b_ref[...])


def _mixer(x, w_in, pool_w, pool_scale, dw_w, dw_b, cn_g, cn_b, w_out, ln_g, ln_b):
    bsz, s, d = x.shape
    tm = MIX_TM
    d_in = w_in.shape[1]
    return pl.pallas_call(
        _mixer_kernel,
        grid=(bsz, s // tm),
        in_specs=[
            pl.BlockSpec((None, tm, d), lambda b, i: (b, i, 0)),
            _const_spec((d, d_in)),
            _const_spec(pool_w.shape),
            _const_spec((1, D_POOL)),
            _const_spec((CONV_KERNEL, D_CONV)),
            _const_spec((1, D_CONV)),
            _const_spec((1, D_CONV)),
            _const_spec((1, D_CONV)),
            _const_spec((d, d)),
            _const_spec((1, d)),
            _const_spec((1, d)),
        ],
        out_specs=pl.BlockSpec((None, tm, d), lambda b, i: (b, i, 0)),
        out_shape=jax.ShapeDtypeStruct(x.shape, F32),
        scratch_shapes=[
            pltpu.VMEM((tm + POOL_HALO, D_POOL), F32),
            pltpu.VMEM((tm + CONV_HALO, D_CONV), F32),
            pltpu.VMEM((SUBLANES - 1, tm + CONV_HALO - SUBLANES, D_CONV), F32),
            pltpu.VMEM((tm, D_CONV), F32),
            pltpu.VMEM((tm, d), BF16),
        ],
        compiler_params=_params(("arbitrary", "arbitrary")),
        name="mixer0",
    )(x, w_in, pool_w, pool_scale, dw_w, dw_b, cn_g, cn_b, w_out, ln_g, ln_b)


def _ffn_kernel(*refs, with_attn_out):
    if with_attn_out:
        ht_ref, wo_ref, lnm_g_ref, lnm_b_ref = refs[:4]
        refs = refs[4:]
    (x_ref, p_ref, wup_ref, dw_w_ref, dw_b_ref, wd_ref, wproj_ref, wpg_ref, bpg_ref,
     ln_g_ref, ln_b_ref, o_ref, carry) = refs
    i = pl.program_id(1)
    tm, ch = FFN_TM, FFN_CH
    n_ch = D_FF // ch

    @pl.when(i == 0)
    def _():
        carry[...] = jnp.zeros(carry.shape, F32)

    def shifted(g, h, back):
        head_row = lax.broadcasted_iota(jnp.int32, h.shape, 0)
        head = jnp.where(head_row < back, pltpu.roll(h, back, axis=0),
                         pltpu.roll(g, back, axis=0)[:FFN_HALO])
        return jnp.concatenate([head, pltpu.roll(g, back, axis=0)[FFN_HALO:]], axis=0)

    n_sub = x_ref.shape[0] // tm
    inputs = []
    for sub in range(n_sub):
        rows = slice(sub * tm, (sub + 1) * tm)
        xb = x_ref[rows, :]
        if with_attn_out:
            mix = lax.dot_general(ht_ref[:, rows], wo_ref[...], _TN, preferred_element_type=F32)
            xb = _layer_norm(DEEPNORM_ALPHA * xb + mix, lnm_g_ref[...], lnm_b_ref[...])
        inputs.append(xb)

    for sub in range(n_sub):
        rows = slice(sub * tm, (sub + 1) * tm)
        xb = inputs[sub]
        xb16 = xb.astype(BF16)

        def up_proj(c):
            gate = jnp.dot(xb16, wup_ref[:, c * ch:(c + 1) * ch], preferred_element_type=F32)
            h = carry[c]
            carry[c] = gate[tm - FFN_HALO:, :]
            val = jnp.dot(xb16, wup_ref[:, D_FF + c * ch:D_FF + (c + 1) * ch],
                          preferred_element_type=F32)
            return gate, h, val

        acc = None
        nxt = up_proj(0)
        for c in range(n_ch):
            cols = slice(c * ch, (c + 1) * ch)
            g, h, val = nxt
            if c + 1 < n_ch:
                nxt = up_proj(c + 1)
            conv = dw_b_ref[:, cols] + g * dw_w_ref[FFN_CONV_KERNEL - 1:FFN_CONV_KERNEL, cols]
            for k in range(FFN_CONV_KERNEL - 1):
                conv = conv + shifted(g, h, FFN_CONV_KERNEL - 1 - k) * dw_w_ref[k:k + 1, cols]
            act = (_gelu_tanh(conv) * val).astype(BF16)
            part = jnp.dot(act, wd_ref[cols, :], preferred_element_type=F32)
            acc = part if acc is None else acc + part

        gate_p = jax.nn.sigmoid(
            jnp.dot(xb16, wpg_ref[...], preferred_element_type=F32) + bpg_ref[...])
        ple = gate_p * jnp.dot(p_ref[rows, :].astype(BF16), wproj_ref[...],
                               preferred_element_type=F32)
        y = DEEPNORM_ALPHA * xb + acc + ple
        o_ref[rows, :] = _layer_norm(y, ln_g_ref[...], ln_b_ref[...])


def _ffn(x, p, layer, w_up, dw_w, dw_b, w_down, w_proj, w_pgate, b_pgate, ln_g, ln_b,
         attn_out=None):
    bsz, s, d = x.shape
    tb = FFN_TM * FFN_SUB
    in_specs = [
        pl.BlockSpec((None, tb, d), lambda b, i: (b, i, 0)),
        pl.BlockSpec((None, None, tb, PLE_DIM), lambda b, i: (layer, b, i, 0)),
        _const_spec((d, 2 * D_FF)),
        _const_spec((FFN_CONV_KERNEL, D_FF)),
        _const_spec((1, D_FF)),
        _const_spec((D_FF, d)),
        _const_spec((PLE_DIM, d)),
        _const_spec((d, d)),
        _const_spec((1, d)),
        _const_spec((1, d)),
        _const_spec((1, d)),
    ]
    args = (x, p, w_up, dw_w, dw_b, w_down, w_proj, w_pgate, b_pgate, ln_g, ln_b)
    if attn_out is not None:
        in_specs = [pl.BlockSpec((None, d, tb), lambda b, i: (b, 0, i)), _const_spec((d, d)),
                    _const_spec((1, d)), _const_spec((1, d))] + in_specs
        args = tuple(attn_out) + args
    return pl.pallas_call(
        functools.partial(_ffn_kernel, with_attn_out=attn_out is not None),
        grid=(bsz, s // tb),
        in_specs=in_specs,
        out_specs=pl.BlockSpec((None, tb, d), lambda b, i: (b, i, 0)),
        out_shape=jax.ShapeDtypeStruct(x.shape, F32),
        scratch_shapes=[pltpu.VMEM((D_FF // FFN_CH, FFN_HALO, FFN_CH), F32)],
        compiler_params=_params(("arbitrary", "arbitrary")),
        name="convffn",
    )(*args)


def _qkv_kernel(x_ref, wqt_ref, wk_ref, wvt_ref, qt_ref, k_ref, vt_ref):
    xb16 = x_ref[...].astype(BF16)
    qt_ref[...] = lax.dot_general(wqt_ref[...], xb16, _NT,
                                  preferred_element_type=F32).astype(BF16)
    k_ref[...] = jnp.dot(xb16, wk_ref[...], preferred_element_type=F32).astype(BF16)
    vt_ref[...] = lax.dot_general(wvt_ref[...], xb16, _NT,
                                  preferred_element_type=F32).astype(BF16)


def _qkv(x, wqt, wk, wvt):
    bsz, s, d = x.shape
    tm = QKV_TM
    row_spec = pl.BlockSpec((None, tm, d), lambda b, i: (b, i, 0))
    col_spec = pl.BlockSpec((None, d, tm), lambda b, i: (b, 0, i))
    return pl.pallas_call(
        _qkv_kernel,
        grid=(bsz, s // tm),
        in_specs=[row_spec, _const_spec((d, d)), _const_spec((d, d)), _const_spec((d, d))],
        out_specs=[col_spec, row_spec, col_spec],
        out_shape=[jax.ShapeDtypeStruct((bsz, d, s), BF16),
                   jax.ShapeDtypeStruct((bsz, s, d), BF16),
                   jax.ShapeDtypeStruct((bsz, d, s), BF16)],
        compiler_params=_params(("arbitrary", "arbitrary")),
        name="qkv",
    )(x, wqt, wk, wvt)


def _attn_kernel(qt_ref, k_ref, vt_ref, bias_ref, ot_ref, *s_bufs):
    s_len = k_ref.shape[0]
    tq = ATT_TQ
    n_kt, n_qt = ATT_NK // LANES, tq // LANES
    head_of_row = lax.broadcasted_iota(jnp.int32, (LANES, tq), 0) // HEAD_DIM

    def tile_masked(kt, qt):
        per_tile = LANES // CHUNK
        gap = (qt - kt) * per_tile + LEFT_CHUNKS
        return gap + (per_tile - 1) < 0 or gap - (per_tile - 1) > LEFT_CHUNKS

    def window_start(q0, nk):
        k0 = q0 + tq - nk
        return k0 if isinstance(k0, int) else pl.multiple_of(k0, tq)

    def scores(q0, nk, h):
        kwin = k_ref[pl.ds(window_start(q0, nk), nk), :]
        qblk = qt_ref[:, pl.ds(q0, tq)]
        qh = jnp.where(head_of_row == h, qblk, jnp.zeros_like(qblk))
        first_kt = n_kt - nk // LANES
        bias = jnp.concatenate([
            jnp.concatenate([
                bias_ref[h, :, (qt - kt + n_kt - 1) * LANES:(qt - kt + n_kt) * LANES]
                for qt in range(n_qt)], axis=1)
            for kt in range(first_kt, n_kt)], axis=0)
        return jnp.dot(kwin, qh, preferred_element_type=F32) + bias

    def softmax_pv(q0, nk, h, sc):
        m = jnp.max(sc, axis=0, keepdims=True)
        first_kt = n_kt - nk // LANES
        pr = jnp.concatenate([
            jnp.concatenate([
                jnp.zeros((LANES, LANES), BF16) if tile_masked(kt, qt) else
                jnp.exp2(sc[(kt - first_kt) * LANES:(kt - first_kt + 1) * LANES,
                            qt * LANES:(qt + 1) * LANES]
                         - m[:, qt * LANES:(qt + 1) * LANES]).astype(BF16)
                for qt in range(n_qt)], axis=1)
            for kt in range(first_kt, n_kt)], axis=0)
        vwin = jnp.concatenate(
            [vt_ref[h * HEAD_DIM:(h + 1) * HEAD_DIM, pl.ds(window_start(q0, nk), nk)],
             jnp.ones((BF16_SUBLANES, nk), BF16)], axis=0)
        o = jnp.dot(vwin, pr, preferred_element_type=F32)
        ot_ref[h * HEAD_DIM:(h + 1) * HEAD_DIM, pl.ds(q0, tq)] = (
            o[:HEAD_DIM] / o[HEAD_DIM:HEAD_DIM + 1]).astype(BF16)

    items = [(b, h) for b in range(s_len // tq) for h in range(HEADS_PER_STEP)]

    def window(b):
        return min((b + 1) * tq, ATT_NK)

    def fill(n):
        b, h = items[n]
        s_bufs[n % len(s_bufs)][0:window(b), :] = scores(b * tq, window(b), h)

    for n in range(min(ATT_AHEAD, len(items))):
        fill(n)
    for n, (b, h) in enumerate(items):
        if n + ATT_AHEAD < len(items):
            fill(n + ATT_AHEAD)
        softmax_pv(b * tq, window(b), h, s_bufs[n % len(s_bufs)][0:window(b), :])


def _attention(qt, k, vt, bias_t):
    bsz, d, s = qt.shape
    n_groups = d // LANES
    lane_spec = pl.BlockSpec((None, LANES, s), lambda g, b: (b, g, 0))
    return pl.pallas_call(
        _attn_kernel,
        grid=(n_groups, bsz),
        in_specs=[
            lane_spec,
            pl.BlockSpec((None, s, LANES), lambda g, b: (b, 0, g)),
            lane_spec,
            pl.BlockSpec((HEADS_PER_STEP,) + bias_t.shape[1:], lambda g, b: (g, 0, 0)),
        ],
        out_specs=lane_spec,
        out_shape=jax.ShapeDtypeStruct((bsz, d, s), BF16),
        scratch_shapes=[pltpu.VMEM((ATT_NK, ATT_TQ), F32)] * (ATT_AHEAD + 1),
        compiler_params=_params(("arbitrary", "arbitrary")),
        name="band_attn",
    )(qt, k, vt, bias_t)


def _attn_bias_table(rel_bias):
    n_heads = rel_bias.shape[0]
    n_m = (ATT_NK + ATT_TQ) // LANES - 1
    length = (n_m + 1) * LANES - 1
    dist = jnp.arange(length) - (LANES - 1) - (ATT_NK - LANES - ATT_HIST)
    per_dist = rel_bias[:, jnp.clip(dist, -MAX_REL_DIST, MAX_REL_DIST) + MAX_REL_DIST]
    ext = jnp.pad(per_dist.astype(F32), ((0, 0), (0, 1)))
    skew = jnp.tile(ext, (1, LANES))[:, :LANES * length].reshape(n_heads, LANES, length)
    table = skew[:, :, LANES - 1:]
    a = jnp.arange(LANES)[:, None]
    c = jnp.arange(n_m * LANES)[None, :]
    m, b = c // LANES, c % LANES
    chunk_gap = ((m - (ATT_NK // LANES - 1)) * (LANES // CHUNK) + LEFT_CHUNKS
                 + b // CHUNK - a // CHUNK)
    visible = (chunk_gap >= 0) & (chunk_gap <= LEFT_CHUNKS)
    return jnp.where(visible[None], table * LOG2_E, NEG_INF)


def _row(v):
    return v.reshape(1, -1)


def _cast_kernel(w_ref, o_ref):
    o_ref[...] = w_ref[...].astype(BF16)


def _layer_bf16(w, layer):
    _, r, c = w.shape
    tr = CAST_ROWS
    return pl.pallas_call(
        _cast_kernel,
        grid=(r // tr,),
        in_specs=[pl.BlockSpec((None, tr, c), lambda i: (layer, i, 0))],
        out_specs=pl.BlockSpec((tr, c), lambda i: (i, 0)),
        out_shape=jax.ShapeDtypeStruct((r, c), BF16),
        compiler_params=_params(("arbitrary",)),
        name="cast_bf16",
    )(w)


def kernel(x, p, mix_w_in, pool_w, pool_scale, conv_dw_w, conv_dw_b, conv_ln_g, conv_ln_b,
           mix_w_out, attn_w_qkv, attn_rel_bias, attn_w_o, ln_mix_g, ln_mix_b, ffn_w_up,
           ffn_dw_w, ffn_dw_b, ffn_w_down, ple_w_proj, ple_w_gate, ple_b_gate, ln_ffn_g,
           ln_ffn_b):
    def ffn(xi, i, attn_out=None):
        return _ffn(xi, p, i, _layer_bf16(ffn_w_up, i), ffn_dw_w[i], _row(ffn_dw_b[i]),
                    _layer_bf16(ffn_w_down, i), ple_w_proj[i].astype(BF16),
                    _layer_bf16(ple_w_gate, i), _row(ple_b_gate[i]),
                    _row(ln_ffn_g[i]), _row(ln_ffn_b[i]), attn_out=attn_out)

    x = _mixer(x, mix_w_in[0].astype(BF16), pool_w[0].astype(BF16), _row(pool_scale[0]),
               conv_dw_w[0], _row(conv_dw_b[0]), _row(conv_ln_g[0]), _row(conv_ln_b[0]),
               mix_w_out[0].astype(BF16), _row(ln_mix_g[0]), _row(ln_mix_b[0]))
    x = ffn(x, 0)

    w_qkv = attn_w_qkv[0]
    wqt = (w_qkv[:, :D_MODEL].T * (HEAD_DIM ** -0.5 * LOG2_E)).astype(BF16)
    wk = w_qkv[:, D_MODEL:2 * D_MODEL].astype(BF16)
    wvt = w_qkv[:, 2 * D_MODEL:].T.astype(BF16)
    qt, k, vt = _qkv(x, wqt, wk, wvt)
    heads_t = _attention(qt, k, vt, _attn_bias_table(attn_rel_bias[0]))
    x = ffn(x, 1, attn_out=(heads_t, attn_w_o[0].astype(BF16),
                            _row(ln_mix_g[1]), _row(ln_mix_b[1])))
    return x
```

```python
import functools

import jax
import jax.numpy as jnp
from jax import lax
from jax.experimental import pallas as pl
from jax.experimental.pallas import tpu as pltpu

F32 = jnp.float32
BF16 = jnp.bfloat16

D_MODEL = 1024
DEPTH = 2
CHUNK = 64
PLE_DIM = 256
D_POOL = 512
D_CONV = 512
POOL_WINDOWS = (2, 4, 8, 16)
POOL_GROUP = 128
CONV_KERNEL = 31
HEAD_DIM = 64
N_HEADS = 16
LEFT_CHUNKS = 8
MAX_REL_DIST = 256
D_FF = 2816
FFN_CONV_KERNEL = 3
DEEPNORM_ALPHA = (2 * DEPTH) ** 0.25
LN_EPS = 1e-5
NEG_INF = -1e30
LOG2_E = 1.4426950408889634
GELU_C1 = -2.0 * LOG2_E * 0.7978845608028654
GELU_C3 = GELU_C1 * 0.044715

SUBLANES = 8
BF16_SUBLANES = 16
LANES = 128
VMEM_LIMIT_BYTES = 56 * 1024 * 1024

MIX_TM = 1024
MIX_RC = 64
POOL_HALO = 16
CONV_HALO = 32
FFN_TM = 256
FFN_SUB = 2
FFN_CH = 256
FFN_HALO = SUBLANES
QKV_TM = 512
ATT_TQ = 256
ATT_HIST = LEFT_CHUNKS * CHUNK
ATT_NK = ATT_HIST + ATT_TQ
HEADS_PER_STEP = LANES // HEAD_DIM
ATT_AHEAD = 2
CAST_ROWS = 128

_NT = (((1,), (1,)), ((), ()))
_TN = (((0,), (0,)), ((), ()))


def _layer_norm(v, g, b):
    mu = jnp.mean(v, axis=-1, keepdims=True)
    c = v - mu
    var = jnp.mean(c * c, axis=-1, keepdims=True)
    return c * lax.rsqrt(var + LN_EPS) * g + b


def _gelu_tanh(v):
    u2 = v * (GELU_C1 + GELU_C3 * (v * v))
    return v / (1.0 + jnp.exp2(u2))


def _params(semantics):
    return pltpu.CompilerParams(dimension_semantics=semantics,
                                vmem_limit_bytes=VMEM_LIMIT_BYTES)


def _const_spec(shape):
    zeros = (0,) * len(shape)
    return pl.BlockSpec(shape, lambda *_: zeros, pipeline_mode=pl.Buffered(1))


def _mixer_kernel(x_ref, w_in_ref, pool_w_ref, pool_scale_ref, dw_w_ref, dw_b_ref,
                  cn_g_ref, cn_b_ref, w_out_ref, ln_g_ref, ln_b_ref, o_ref,
                  a_ext, g_ext, g_ph, conv_out, cat):
    i = pl.program_id(1)
    tm = x_ref.shape[0]

    @pl.when(i == 0)
    def _():
        a_ext[0:POOL_HALO, :] = jnp.zeros((POOL_HALO, D_POOL), F32)
        g_ext[0:CONV_HALO, :] = jnp.zeros((CONV_HALO, D_CONV), F32)

    @pl.when(i > 0)
    def _():
        a_ext[0:POOL_HALO, :] = a_ext[tm:tm + POOL_HALO, :]
        g_ext[0:CONV_HALO, :] = g_ext[tm:tm + CONV_HALO, :]

    xb = x_ref[...]
    u = jnp.dot(xb.astype(BF16), w_in_ref[...], preferred_element_type=F32)
    a_ext[POOL_HALO:, :] = u[:, :D_POOL]
    g_ext[CONV_HALO:, :] = (u[:, D_POOL:D_POOL + D_CONV]
                            * jax.nn.sigmoid(u[:, D_POOL + D_CONV:]))

    row = lax.broadcasted_iota(jnp.int32, (tm, POOL_GROUP), 0)
    pos = (row + (i * tm + 1)).astype(F32)
    for g, w in enumerate(POOL_WINDOWS):
        cols = slice(g * POOL_GROUP, (g + 1) * POOL_GROUP)
        ext = a_ext[:, cols]
        cur = ext[POOL_HALO:]
        s, span = ext, 1
        while span < w:
            s = s + pltpu.roll(s, span, axis=0)
            span *= 2
        d = s[POOL_HALO:] / jnp.minimum(pos, float(w)) - cur
        ya = jnp.dot(d.astype(BF16), pool_w_ref[g], preferred_element_type=F32)
        cat[:, cols] = (ya * pool_scale_ref[:, cols]).astype(BF16)

    g_all = g_ext[...]
    n_ext = g_all.shape[0]
    for p in range(1, SUBLANES):
        g_ph[p - 1, :, :] = pltpu.roll(g_all, n_ext - p, axis=0)[:g_ph.shape[1]]

    def conv_chunk(c, carry):
        r0 = pl.multiple_of(c * MIX_RC, MIX_RC)
        acc = None
        for k in range(CONV_KERNEL):
            off = CONV_HALO - (CONV_KERNEL - 1) + k
            q, p = divmod(off, SUBLANES)
            if p == 0:
                t = g_ext[pl.ds(r0 + SUBLANES * q, MIX_RC), :]
            else:
                t = g_ph[p - 1, pl.ds(r0 + SUBLANES * q, MIX_RC), :]
            t = t * dw_w_ref[k:k + 1, :]
            acc = t if acc is None else acc + t
        conv_out[pl.ds(r0, MIX_RC), :] = acc
        return carry

    lax.fori_loop(0, tm // MIX_RC, conv_chunk, 0)
    h = _layer_norm(conv_out[...] + dw_b_ref[...], cn_g_ref[...], cn_b_ref[...])
    cat[:, D_POOL:] = (h * jax.nn.sigmoid(h)).astype(BF16)

    mix = jnp.dot(cat[...], w_out_ref[...], preferred_element_type=F32)
    o_ref[...] = _layer_norm(DEEPNORM_ALPHA * xb + mix, ln_g_ref[...], ln_b_ref[...])


def _mixer(x, w_in, pool_w, pool_scale, dw_w, dw_b, cn_g, cn_b, w_out, ln_g, ln_b):
    bsz, s, d = x.shape
    tm = MIX_TM
    d_in = w_in.shape[1]
    return pl.pallas_call(
        _mixer_kernel,
        grid=(bsz, s // tm),
        in_specs=[
            pl.BlockSpec((None, tm, d), lambda b, i: (b, i, 0)),
            _const_spec((d, d_in)),
            _const_spec(pool_w.shape),
            _const_spec((1, D_POOL)),
            _const_spec((CONV_KERNEL, D_CONV)),
            _const_spec((1, D_CONV)),
            _const_spec((1, D_CONV)),
            _const_spec((1, D_CONV)),
            _const_spec((d, d)),
            _const_spec((1, d)),
            _const_spec((1, d)),
        ],
        out_specs=pl.BlockSpec((None, tm, d), lambda b, i: (b, i, 0)),
        out_shape=jax.ShapeDtypeStruct(x.shape, F32),
        scratch_shapes=[
            pltpu.VMEM((tm + POOL_HALO, D_POOL), F32),
            pltpu.VMEM((tm + CONV_HALO, D_CONV), F32),
            pltpu.VMEM((SUBLANES - 1, tm + CONV_HALO - SUBLANES, D_CONV), F32),
            pltpu.VMEM((tm, D_CONV), F32),
            pltpu.VMEM((tm, d), BF16),
        ],
        compiler_params=_params(("arbitrary", "arbitrary")),
        name="mixer0",
    )(x, w_in, pool_w, pool_scale, dw_w, dw_b, cn_g, cn_b, w_out, ln_g, ln_b)


def _ffn_kernel(*refs, with_attn_out):
    if with_attn_out:
        ht_ref, wo_ref, lnm_g_ref, lnm_b_ref = refs[:4]
        refs = refs[4:]
    (x_ref, p_ref, wup_ref, dw_w_ref, dw_b_ref, wd_ref, wproj_ref, wpg_ref, bpg_ref,
     ln_g_ref, ln_b_ref, o_ref, carry) = refs
    i = pl.program_id(1)
    tm, ch = FFN_TM, FFN_CH
    n_ch = D_FF // ch

    @pl.when(i == 0)
    def _():
        carry[...] = jnp.zeros(carry.shape, F32)

    def shifted(g, h, back):
        head_row = lax.broadcasted_iota(jnp.int32, h.shape, 0)
        head = jnp.where(head_row < back, pltpu.roll(h, back, axis=0),
                         pltpu.roll(g, back, axis=0)[:FFN_HALO])
        return jnp.concatenate([head, pltpu.roll(g, back, axis=0)[FFN_HALO:]], axis=0)

    n_sub = x_ref.shape[0] // tm
    inputs = []
    for sub in range(n_sub):
        rows = slice(sub * tm, (sub + 1) * tm)
        xb = x_ref[rows, :]
        if with_attn_out:
            mix = lax.dot_general(ht_ref[:, rows], wo_ref[...], _TN, preferred_element_type=F32)
            xb = _layer_norm(DEEPNORM_ALPHA * xb + mix, lnm_g_ref[...], lnm_b_ref[...])
        inputs.append(xb)

    for sub in range(n_sub):
        rows = slice(sub * tm, (sub + 1) * tm)
        xb = inputs[sub]
        xb16 = xb.astype(BF16)

        def up_proj(c):
            gate = jnp.dot(xb16, wup_ref[:, c * ch:(c + 1) * ch], preferred_element_type=F32)
            h = carry[c]
            carry[c] = gate[tm - FFN_HALO:, :]
            val = jnp.dot(xb16, wup_ref[:, D_FF + c * ch:D_FF + (c + 1) * ch],
                          preferred_element_type=F32)
            return gate, h, val

        acc = None
        nxt = up_proj(0)
        for c in range(n_ch):
            cols = slice(c * ch, (c + 1) * ch)
            g, h, val = nxt
            if c + 1 < n_ch:
                nxt = up_proj(c + 1)
            conv = dw_b_ref[:, cols] + g * dw_w_ref[FFN_CONV_KERNEL - 1:FFN_CONV_KERNEL, cols]
            for k in range(FFN_CONV_KERNEL - 1):
                conv = conv + shifted(g, h, FFN_CONV_KERNEL - 1 - k) * dw_w_ref[k:k + 1, cols]
            act = (_gelu_tanh(conv) * val).astype(BF16)
            part = jnp.dot(act, wd_ref[cols, :], preferred_element_type=F32)
            acc = part if acc is None else acc + part

        gate_p = jax.nn.sigmoid(
            jnp.dot(xb16, wpg_ref[...], preferred_element_type=F32) + bpg_ref[...])
        ple = gate_p * jnp.dot(p_ref[rows, :].astype(BF16), wproj_ref[...],
                               preferred_element_type=F32)
        y = DEEPNORM_ALPHA * xb + acc + ple
        o_ref[rows, :] = _layer_norm(y, ln_g_ref[...], ln_b_ref[...])


def _ffn(x, p, layer, w_up, dw_w, dw_b, w_down, w_proj, w_pgate, b_pgate, ln_g, ln_b,
         attn_out=None):
    bsz, s, d = x.shape
    tb = FFN_TM * FFN_SUB
    in_specs = [
        pl.BlockSpec((None, tb, d), lambda b, i: (b, i, 0)),
        pl.BlockSpec((None, None, tb, PLE_DIM), lambda b, i: (layer, b, i, 0)),
        _const_spec((d, 2 * D_FF)),
        _const_spec((FFN_CONV_KERNEL, D_FF)),
        _const_spec((1, D_FF)),
        _const_spec((D_FF, d)),
        _const_spec((PLE_DIM, d)),
        _const_spec((d, d)),
        _const_spec((1, d)),
        _const_spec((1, d)),
        _const_spec((1, d)),
    ]
    args = (x, p, w_up, dw_w, dw_b, w_down, w_proj, w_pgate, b_pgate, ln_g, ln_b)
    if attn_out is not None:
        in_specs = [pl.BlockSpec((None, d, tb), lambda b, i: (b, 0, i)), _const_spec((d, d)),
                    _const_spec((1, d)), _const_spec((1, d))] + in_specs
        args = tuple(attn_out) + args
    return pl.pallas_call(
        functools.partial(_ffn_kernel, with_attn_out=attn_out is not None),
        grid=(bsz, s // tb),
        in_specs=in_specs,
        out_specs=pl.BlockSpec((None, tb, d), lambda b, i: (b, i, 0)),
        out_shape=jax.ShapeDtypeStruct(x.shape, F32),
        scratch_shapes=[pltpu.VMEM((D_FF // FFN_CH, FFN_HALO, FFN_CH), F32)],
        compiler_params=_params(("arbitrary", "arbitrary")),
        name="convffn",
    )(*args)


def _qkv_kernel(x_ref, wqt_ref, wk_ref, wvt_ref, qt_ref, k_ref, vt_ref):
    xb16 = x_ref[...].astype(BF16)
    qt_ref[...] = lax.dot_general(wqt_ref[...], xb16, _NT,
                                  preferred_element_type=F32).astype(BF16)
    k_ref[...] = jnp.dot(xb16, wk_ref[...], preferred_element_type=F32).astype(BF16)
    vt_ref[...] = lax.dot_general(wvt_ref[...], xb16, _NT,
                                  preferred_element_type=F32).astype(BF16)


def _qkv(x, wqt, wk, wvt):
    bsz, s, d = x.shape
    tm = QKV_TM
    row_spec = pl.BlockSpec((None, tm, d), lambda b, i: (b, i, 0))
    col_spec = pl.BlockSpec((None, d, tm), lambda b, i: (b, 0, i))
    return pl.pallas_call(
        _qkv_kernel,
        grid=(bsz, s // tm),
        in_specs=[row_spec, _const_spec((d, d)), _const_spec((d, d)), _const_spec((d, d))],
        out_specs=[col_spec, row_spec, col_spec],
        out_shape=[jax.ShapeDtypeStruct((bsz, d, s), BF16),
                   jax.ShapeDtypeStruct((bsz, s, d), BF16),
                   jax.ShapeDtypeStruct((bsz, d, s), BF16)],
        compiler_params=_params(("arbitrary", "arbitrary")),
        name="qkv",
    )(x, wqt, wk, wvt)


def _attn_kernel(qt_ref, k_ref, vt_ref, bias_ref, ot_ref, *s_bufs):
    s_len = k_ref.shape[0]
    tq = ATT_TQ
    n_kt, n_qt = ATT_NK // LANES, tq // LANES
    head_of_row = lax.broadcasted_iota(jnp.int32, (LANES, tq), 0) // HEAD_DIM

    def tile_masked(kt, qt):
        per_tile = LANES // CHUNK
        gap = (qt - kt) * per_tile + LEFT_CHUNKS
        return gap + (per_tile - 1) < 0 or gap - (per_tile - 1) > LEFT_CHUNKS

    def window_start(q0, nk):
        k0 = q0 + tq - nk
        return k0 if isinstance(k0, int) else pl.multiple_of(k0, tq)

    def scores(q0, nk, h):
        kwin = k_ref[pl.ds(window_start(q0, nk), nk), :]
        qblk = qt_ref[:, pl.ds(q0, tq)]
        qh = jnp.where(head_of_row == h, qblk, jnp.zeros_like(qblk))
        first_kt = n_kt - nk // LANES
        bias = jnp.concatenate([
            jnp.concatenate([
                bias_ref[h, :, (qt - kt + n_kt - 1) * LANES:(qt - kt + n_kt) * LANES]
                for qt in range(n_qt)], axis=1)
            for kt in range(first_kt, n_kt)], axis=0)
        return jnp.dot(kwin, qh, preferred_element_type=F32) + bias

    def softmax_pv(q0, nk, h, sc):
        m = jnp.max(sc, axis=0, keepdims=True)
        first_kt = n_kt - nk // LANES
        pr = jnp.concatenate([
            jnp.concatenate([
                jnp.zeros((LANES, LANES), BF16) if tile_masked(kt, qt) else
                jnp.exp2(sc[(kt - first_kt) * LANES:(kt - first_kt + 1) * LANES,
                            qt * LANES:(qt + 1) * LANES]
                         - m[:, qt * LANES:(qt + 1) * LANES]).astype(BF16)
                for qt in range(n_qt)], axis=1)
            for kt in range(first_kt, n_kt)], axis=0)
        vwin = jnp.concatenate(
            [vt_ref[h * HEAD_DIM:(h + 1) * HEAD_DIM, pl.ds(window_start(q0, nk), nk)],
             jnp.ones((BF16_SUBLANES, nk), BF16)], axis=0)
        o = jnp.dot(vwin, pr, preferred_element_type=F32)
        ot_ref[h * HEAD_DIM:(h + 1) * HEAD_DIM, pl.ds(q0, tq)] = (
            o[:HEAD_DIM] / o[HEAD_DIM:HEAD_DIM + 1]).astype(BF16)

    items = [(b, h) for b in range(s_len // tq) for h in range(HEADS_PER_STEP)]

    def window(b):
        return min((b + 1) * tq, ATT_NK)

    def fill(n):
        b, h = items[n]
        s_bufs[n % len(s_bufs)][0:window(b), :] = scores(b * tq, window(b), h)

    for n in range(min(ATT_AHEAD, len(items))):
        fill(n)
    for n, (b, h) in enumerate(items):
        if n + ATT_AHEAD < len(items):
            fill(n + ATT_AHEAD)
        softmax_pv(b * tq, window(b), h, s_bufs[n % len(s_bufs)][0:window(b), :])


def _attention(qt, k, vt, bias_t):
    bsz, d, s = qt.shape
    n_groups = d // LANES
    lane_spec = pl.BlockSpec((None, LANES, s), lambda g, b: (b, g, 0))
    return pl.pallas_call(
        _attn_kernel,
        grid=(n_groups, bsz),
        in_specs=[
            lane_spec,
            pl.BlockSpec((None, s, LANES), lambda g, b: (b, 0, g)),
            lane_spec,
            pl.BlockSpec((HEADS_PER_STEP,) + bias_t.shape[1:], lambda g, b: (g, 0, 0)),
        ],
        out_specs=lane_spec,
        out_shape=jax.ShapeDtypeStruct((bsz, d, s), BF16),
        scratch_shapes=[pltpu.VMEM((ATT_NK, ATT_TQ), F32)] * (ATT_AHEAD + 1),
        compiler_params=_params(("arbitrary", "arbitrary")),
        name="band_attn",
    )(qt, k, vt, bias_t)


def _attn_bias_table(rel_bias):
    n_heads = rel_bias.shape[0]
    n_m = (ATT_NK + ATT_TQ) // LANES - 1
    length = (n_m + 1) * LANES - 1
    dist = jnp.arange(length) - (LANES - 1) - (ATT_NK - LANES - ATT_HIST)
    per_dist = rel_bias[:, jnp.clip(dist, -MAX_REL_DIST, MAX_REL_DIST) + MAX_REL_DIST]
    ext = jnp.pad(per_dist.astype(F32), ((0, 0), (0, 1)))
    skew = jnp.tile(ext, (1, LANES))[:, :LANES * length].reshape(n_heads, LANES, length)
    table = skew[:, :, LANES - 1:]
    a = jnp.arange(LANES)[:, None]
    c = jnp.arange(n_m * LANES)[None, :]
    m, b = c // LANES, c % LANES
    chunk_gap = ((m - (ATT_NK // LANES - 1)) * (LANES // CHUNK) + LEFT_CHUNKS
                 + b // CHUNK - a // CHUNK)
    visible = (chunk_gap >= 0) & (chunk_gap <= LEFT_CHUNKS)
    return jnp.where(visible[None], table * LOG2_E, NEG_INF)


def _row(v):
    return v.reshape(1, -1)


def _cast_kernel(w_ref, o_ref):
    o_ref[...] = w_ref[...].astype(BF16)


def _layer_bf16(w, layer):
    _, r, c = w.shape
    tr = CAST_ROWS
    return pl.pallas_call(
        _cast_kernel,
        grid=(r // tr,),
        in_specs=[pl.BlockSpec((None, tr, c), lambda i: (layer, i, 0))],
        out_specs=pl.BlockSpec((tr, c), lambda i: (i, 0)),
        out_shape=jax.ShapeDtypeStruct((r, c), BF16),
        compiler_params=_params(("arbitrary",)),
        name="cast_bf16",
    )(w)


def kernel(x, p, mix_w_in, pool_w, pool_scale, conv_dw_w, conv_dw_b, conv_ln_g, conv_ln_b,
           mix_w_out, attn_w_qkv, attn_rel_bias, attn_w_o, ln_mix_g, ln_mix_b, ffn_w_up,
           ffn_dw_w, ffn_dw_b, ffn_w_down, ple_w_proj, ple_w_gate, ple_b_gate, ln_ffn_g,
           ln_ffn_b):
    def ffn(xi, i, attn_out=None):
        return _ffn(xi, p, i, _layer_bf16(ffn_w_up, i), ffn_dw_w[i], _row(ffn_dw_b[i]),
                    _layer_bf16(ffn_w_down, i), ple_w_proj[i].astype(BF16),
                    _layer_bf16(ple_w_gate, i), _row(ple_b_gate[i]),
                    _row(ln_ffn_g[i]), _row(ln_ffn_b[i]), attn_out=attn_out)

    x = _mixer(x, mix_w_in[0].astype(BF16), pool_w[0].astype(BF16), _row(pool_scale[0]),
               conv_dw_w[0], _row(conv_dw_b[0]), _row(conv_ln_g[0]), _row(conv_ln_b[0]),
               mix_w_out[0].astype(BF16), _row(ln_mix_g[0]), _row(ln_mix_b[0]))
    x = ffn(x, 0)

    w_qkv = attn_w_qkv[0]
    wqt = (w_qkv[:, :D_MODEL].T * (HEAD_DIM ** -0.5 * LOG2_E)).astype(BF16)
    wk = w_qkv[:, D_MODEL:2 * D_MODEL].astype(BF16)
    wvt = w_qkv[:, 2 * D_MODEL:].T.astype(BF16)
    qt, k, vt = _qkv(x, wqt, wk, wvt)
    heads_t = _attention(qt, k, vt, _attn_bias_table(attn_rel_bias[0]))
    x = ffn(x, 1, attn_out=(heads_t, attn_w_o[0].astype(BF16),
                            _row(ln_mix_g[1]), _row(ln_mix_b[1])))
    return x
```

```python
import functools

import jax
import jax.numpy as jnp
from jax import lax
from jax.experimental import pallas as pl
from jax.experimental.pallas import tpu as pltpu

F32 = jnp.float32
BF16 = jnp.bfloat16

D_MODEL = 1024
DEPTH = 2
CHUNK = 64
PLE_DIM = 256
D_POOL = 512
D_CONV = 512
POOL_WINDOWS = (2, 4, 8, 16)
POOL_GROUP = 128
CONV_KERNEL = 31
HEAD_DIM = 64
N_HEADS = 16
LEFT_CHUNKS = 8
MAX_REL_DIST = 256
D_FF = 2816
FFN_CONV_KERNEL = 3
DEEPNORM_ALPHA = (2 * DEPTH) ** 0.25
LN_EPS = 1e-5
NEG_INF = -1e30
LOG2_E = 1.4426950408889634
GELU_C1 = -2.0 * LOG2_E * 0.7978845608028654
GELU_C3 = GELU_C1 * 0.044715

SUBLANES = 8
BF16_SUBLANES = 16
LANES = 128
VMEM_LIMIT_BYTES = 56 * 1024 * 1024

MIX_TM = 1024
MIX_RC = 64
POOL_HALO = 16
CONV_HALO = 32
FFN_TM = 256
FFN_SUB = 2
FFN_CH = 256
FFN_HALO = SUBLANES
QKV_TM = 512
ATT_TQ = 256
ATT_HIST = LEFT_CHUNKS * CHUNK
ATT_NK = ATT_HIST + ATT_TQ
HEADS_PER_STEP = LANES // HEAD_DIM
ATT_AHEAD = 2
CAST_ROWS = 256

_NT = (((1,), (1,)), ((), ()))
_TN = (((0,), (0,)), ((), ()))


def _layer_norm(v, g, b):
    mu = jnp.mean(v, axis=-1, keepdims=True)
    c = v - mu
    var = jnp.mean(c * c, axis=-1, keepdims=True)
    return c * lax.rsqrt(var + LN_EPS) * g + b


def _gelu_tanh(v):
    u2 = v * (GELU_C1 + GELU_C3 * (v * v))
    return v / (1.0 + jnp.exp2(u2))


def _params(semantics):
    return pltpu.CompilerParams(dimension_semantics=semantics,
                                vmem_limit_bytes=VMEM_LIMIT_BYTES)


def _const_spec(shape):
    zeros = (0,) * len(shape)
    return pl.BlockSpec(shape, lambda *_: zeros, pipeline_mode=pl.Buffered(1))


def _mixer_kernel(x_ref, w_in_ref, pool_w_ref, pool_scale_ref, dw_w_ref, dw_b_ref,
                  cn_g_ref, cn_b_ref, w_out_ref, ln_g_ref, ln_b_ref, o_ref,
                  a_ext, g_ext, g_ph, conv_out, cat):
    i = pl.program_id(1)
    tm = x_ref.shape[0]

    @pl.when(i == 0)
    def _():
        a_ext[0:POOL_HALO, :] = jnp.zeros((POOL_HALO, D_POOL), F32)
        g_ext[0:CONV_HALO, :] = jnp.zeros((CONV_HALO, D_CONV), F32)

    @pl.when(i > 0)
    def _():
        a_ext[0:POOL_HALO, :] = a_ext[tm:tm + POOL_HALO, :]
        g_ext[0:CONV_HALO, :] = g_ext[tm:tm + CONV_HALO, :]

    xb = x_ref[...]
    u = jnp.dot(xb.astype(BF16), w_in_ref[...], preferred_element_type=F32)
    a_ext[POOL_HALO:, :] = u[:, :D_POOL]
    g_ext[CONV_HALO:, :] = (u[:, D_POOL:D_POOL + D_CONV]
                            * jax.nn.sigmoid(u[:, D_POOL + D_CONV:]))

    row = lax.broadcasted_iota(jnp.int32, (tm, POOL_GROUP), 0)
    pos = (row + (i * tm + 1)).astype(F32)
    for g, w in enumerate(POOL_WINDOWS):
        cols = slice(g * POOL_GROUP, (g + 1) * POOL_GROUP)
        ext = a_ext[:, cols]
        cur = ext[POOL_HALO:]
        s, span = ext, 1
        while span < w:
            s = s + pltpu.roll(s, span, axis=0)
            span *= 2
        d = s[POOL_HALO:] / jnp.minimum(pos, float(w)) - cur
        ya = jnp.dot(d.astype(BF16), pool_w_ref[g], preferred_element_type=F32)
        cat[:, cols] = (ya * pool_scale_ref[:, cols]).astype(BF16)

    g_all = g_ext[...]
    n_ext = g_all.shape[0]
    for p in range(1, SUBLANES):
        g_ph[p - 1, :, :] = pltpu.roll(g_all, n_ext - p, axis=0)[:g_ph.shape[1]]

    def conv_chunk(c, carry):
        r0 = pl.multiple_of(c * MIX_RC, MIX_RC)
        acc = None
        for k in range(CONV_KERNEL):
            off = CONV_HALO - (CONV_KERNEL - 1) + k
            q, p = divmod(off, SUBLANES)
            if p == 0:
                t = g_ext[pl.ds(r0 + SUBLANES * q, MIX_RC), :]
            else:
                t = g_ph[p - 1, pl.ds(r0 + SUBLANES * q, MIX_RC), :]
            t = t * dw_w_ref[k:k + 1, :]
            acc = t if acc is None else acc + t
        conv_out[pl.ds(r0, MIX_RC), :] = acc
        return carry

    lax.fori_loop(0, tm // MIX_RC, conv_chunk, 0)
    h = _layer_norm(conv_out[...] + dw_b_ref[...], cn_g_ref[...], cn_b_ref[...])
    cat[:, D_POOL:] = (h * jax.nn.sigmoid(h)).astype(BF16)

    mix = jnp.dot(cat[...], w_out_ref[...], preferred_element_type=F32)
    o_ref[...] = _layer_norm(DEEPNORM_ALPHA * xb + mix, ln_g_ref[...], ln_b_ref[...])


def _mixer(x, w_in, pool_w, pool_scale, dw_w, dw_b, cn_g, cn_b, w_out, ln_g, ln_b):
    bsz, s, d = x.shape
    tm = MIX_TM
    d_in = w_in.shape[1]
    return pl.pallas_call(
        _mixer_kernel,
        grid=(bsz, s // tm),
        in_specs=[
            pl.BlockSpec((None, tm, d), lambda b, i: (b, i, 0)),
            _const_spec((d, d_in)),
            _const_spec(pool_w.shape),
            _const_spec((1, D_POOL)),
            _const_spec((CONV_KERNEL, D_CONV)),
            _const_spec((1, D_CONV)),
            _const_spec((1, D_CONV)),
            _const_spec((1, D_CONV)),
            _const_spec((d, d)),
            _const_spec((1, d)),
            _const_spec((1, d)),
        ],
        out_specs=pl.BlockSpec((None, tm, d), lambda b, i: (b, i, 0)),
        out_shape=jax.ShapeDtypeStruct(x.shape, F32),
        scratch_shapes=[
            pltpu.VMEM((tm + POOL_HALO, D_POOL), F32),
            pltpu.VMEM((tm + CONV_HALO, D_CONV), F32),
            pltpu.VMEM((SUBLANES - 1, tm + CONV_HALO - SUBLANES, D_CONV), F32),
            pltpu.VMEM((tm, D_CONV), F32),
            pltpu.VMEM((tm, d), BF16),
        ],
        compiler_params=_params(("arbitrary", "arbitrary")),
        name="mixer0",
    )(x, w_in, pool_w, pool_scale, dw_w, dw_b, cn_g, cn_b, w_out, ln_g, ln_b)


def _ffn_kernel(*refs, with_attn_out):
    if with_attn_out:
        ht_ref, wo_ref, lnm_g_ref, lnm_b_ref = refs[:4]
        refs = refs[4:]
    (x_ref, p_ref, wup_ref, dw_w_ref, dw_b_ref, wd_ref, wproj_ref, wpg_ref, bpg_ref,
     ln_g_ref, ln_b_ref, o_ref, carry) = refs
    i = pl.program_id(1)
    tm, ch = FFN_TM, FFN_CH
    n_ch = D_FF // ch

    @pl.when(i == 0)
    def _():
        carry[...] = jnp.zeros(carry.shape, F32)

    def shifted(g, h, back):
        head_row = lax.broadcasted_iota(jnp.int32, h.shape, 0)
        head = jnp.where(head_row < back, pltpu.roll(h, back, axis=0),
                         pltpu.roll(g, back, axis=0)[:FFN_HALO])
        return jnp.concatenate([head, pltpu.roll(g, back, axis=0)[FFN_HALO:]], axis=0)

    n_sub = x_ref.shape[0] // tm
    inputs = []
    for sub in range(n_sub):
        rows = slice(sub * tm, (sub + 1) * tm)
        xb = x_ref[rows, :]
        if with_attn_out:
            mix = lax.dot_general(ht_ref[:, rows], wo_ref[...], _TN, preferred_element_type=F32)
            xb = _layer_norm(DEEPNORM_ALPHA * xb + mix, lnm_g_ref[...], lnm_b_ref[...])
        inputs.append(xb)

    for sub in range(n_sub):
        rows = slice(sub * tm, (sub + 1) * tm)
        xb = inputs[sub]
        xb16 = xb.astype(BF16)

        def up_proj(c):
            gate = jnp.dot(xb16, wup_ref[:, c * ch:(c + 1) * ch], preferred_element_type=F32)
            h = carry[c]
            carry[c] = gate[tm - FFN_HALO:, :]
            val = jnp.dot(xb16, wup_ref[:, D_FF + c * ch:D_FF + (c + 1) * ch],
                          preferred_element_type=F32)
            return gate, h, val

        acc = None
        nxt = up_proj(0)
        for c in range(n_ch):
            cols = slice(c * ch, (c + 1) * ch)
            g, h, val = nxt
            if c + 1 < n_ch:
                nxt = up_proj(c + 1)
            conv = dw_b_ref[:, cols] + g * dw_w_ref[FFN_CONV_KERNEL - 1:FFN_CONV_KERNEL, cols]
            for k in range(FFN_CONV_KERNEL - 1):
                conv = conv + shifted(g, h, FFN_CONV_KERNEL - 1 - k) * dw_w_ref[k:k + 1, cols]
            act = (_gelu_tanh(conv) * val).astype(BF16)
            part = jnp.dot(act, wd_ref[cols, :], preferred_element_type=F32)
            acc = part if acc is None else acc + part

        gate_p = jax.nn.sigmoid(
            jnp.dot(xb16, wpg_ref[...], preferred_element_type=F32) + bpg_ref[...])
        ple = gate_p * jnp.dot(p_ref[rows, :].astype(BF16), wproj_ref[...],
                               preferred_element_type=F32)
        y = DEEPNORM_ALPHA * xb + acc + ple
        o_ref[rows, :] = _layer_norm(y, ln_g_ref[...], ln_b_ref[...])


def _ffn(x, p, layer, w_up, dw_w, dw_b, w_down, w_proj, w_pgate, b_pgate, ln_g, ln_b,
         attn_out=None):
    bsz, s, d = x.shape
    tb = FFN_TM * FFN_SUB
    in_specs = [
        pl.BlockSpec((None, tb, d), lambda b, i: (b, i, 0)),
        pl.BlockSpec((None, None, tb, PLE_DIM), lambda b, i: (layer, b, i, 0)),
        _const_spec((d, 2 * D_FF)),
        _const_spec((FFN_CONV_KERNEL, D_FF)),
        _const_spec((1, D_FF)),
        _const_spec((D_FF, d)),
        _const_spec((PLE_DIM, d)),
        _const_spec((d, d)),
        _const_spec((1, d)),
        _const_spec((1, d)),
        _const_spec((1, d)),
    ]
    args = (x, p, w_up, dw_w, dw_b, w_down, w_proj, w_pgate, b_pgate, ln_g, ln_b)
    if attn_out is not None:
        in_specs = [pl.BlockSpec((None, d, tb), lambda b, i: (b, 0, i)), _const_spec((d, d)),
                    _const_spec((1, d)), _const_spec((1, d))] + in_specs
        args = tuple(attn_out) + args
    return pl.pallas_call(
        functools.partial(_ffn_kernel, with_attn_out=attn_out is not None),
        grid=(bsz, s // tb),
        in_specs=in_specs,
        out_specs=pl.BlockSpec((None, tb, d), lambda b, i: (b, i, 0)),
        out_shape=jax.ShapeDtypeStruct(x.shape, F32),
        scratch_shapes=[pltpu.VMEM((D_FF // FFN_CH, FFN_HALO, FFN_CH), F32)],
        compiler_params=_params(("arbitrary", "arbitrary")),
        name="convffn",
    )(*args)


def _qkv_kernel(x_ref, wqt_ref, wk_ref, wvt_ref, qt_ref, k_ref, vt_ref):
    xb16 = x_ref[...].astype(BF16)
    qt_ref[...] = lax.dot_general(wqt_ref[...], xb16, _NT,
                                  preferred_element_type=F32).astype(BF16)
    k_ref[...] = jnp.dot(xb16, wk_ref[...], preferred_element_type=F32).astype(BF16)
    vt_ref[...] = lax.dot_general(wvt_ref[...], xb16, _NT,
                                  preferred_element_type=F32).astype(BF16)


def _qkv(x, wqt, wk, wvt):
    bsz, s, d = x.shape
    tm = QKV_TM
    row_spec = pl.BlockSpec((None, tm, d), lambda b, i: (b, i, 0))
    col_spec = pl.BlockSpec((None, d, tm), lambda b, i: (b, 0, i))
    return pl.pallas_call(
        _qkv_kernel,
        grid=(bsz, s // tm),
        in_specs=[row_spec, _const_spec((d, d)), _const_spec((d, d)), _const_spec((d, d))],
        out_specs=[col_spec, row_spec, col_spec],
        out_shape=[jax.ShapeDtypeStruct((bsz, d, s), BF16),
                   jax.ShapeDtypeStruct((bsz, s, d), BF16),
                   jax.ShapeDtypeStruct((bsz, d, s), BF16)],
        compiler_params=_params(("arbitrary", "arbitrary")),
        name="qkv",
    )(x, wqt, wk, wvt)


def _attn_kernel(qt_ref, k_ref, vt_ref, bias_ref, ot_ref, *s_bufs):
    s_len = k_ref.shape[0]
    tq = ATT_TQ
    n_kt, n_qt = ATT_NK // LANES, tq // LANES
    head_of_row = lax.broadcasted_iota(jnp.int32, (LANES, tq), 0) // HEAD_DIM

    def tile_masked(kt, qt):
        per_tile = LANES // CHUNK
        gap = (qt - kt) * per_tile + LEFT_CHUNKS
        return gap + (per_tile - 1) < 0 or gap - (per_tile - 1) > LEFT_CHUNKS

    def window_start(q0, nk):
        k0 = q0 + tq - nk
        return k0 if isinstance(k0, int) else pl.multiple_of(k0, tq)

    def scores(q0, nk, h):
        kwin = k_ref[pl.ds(window_start(q0, nk), nk), :]
        qblk = qt_ref[:, pl.ds(q0, tq)]
        qh = jnp.where(head_of_row == h, qblk, jnp.zeros_like(qblk))
        first_kt = n_kt - nk // LANES
        bias = jnp.concatenate([
            jnp.concatenate([
                bias_ref[h, :, (qt - kt + n_kt - 1) * LANES:(qt - kt + n_kt) * LANES]
                for qt in range(n_qt)], axis=1)
            for kt in range(first_kt, n_kt)], axis=0)
        return jnp.dot(kwin, qh, preferred_element_type=F32) + bias

    def softmax_pv(q0, nk, h, sc):
        m = jnp.max(sc, axis=0, keepdims=True)
        first_kt = n_kt - nk // LANES
        pr = jnp.concatenate([
            jnp.concatenate([
                jnp.zeros((LANES, LANES), BF16) if tile_masked(kt, qt) else
                jnp.exp2(sc[(kt - first_kt) * LANES:(kt - first_kt + 1) * LANES,
                            qt * LANES:(qt + 1) * LANES]
                         - m[:, qt * LANES:(qt + 1) * LANES]).astype(BF16)
                for qt in range(n_qt)], axis=1)
            for kt in range(first_kt, n_kt)], axis=0)
        vwin = jnp.concatenate(
            [vt_ref[h * HEAD_DIM:(h + 1) * HEAD_DIM, pl.ds(window_start(q0, nk), nk)],
             jnp.ones((BF16_SUBLANES, nk), BF16)], axis=0)
        o = jnp.dot(vwin, pr, preferred_element_type=F32)
        ot_ref[h * HEAD_DIM:(h + 1) * HEAD_DIM, pl.ds(q0, tq)] = (
            o[:HEAD_DIM] / o[HEAD_DIM:HEAD_DIM + 1]).astype(BF16)

    items = [(b, h) for b in range(s_len // tq) for h in range(HEADS_PER_STEP)]

    def window(b):
        return min((b + 1) * tq, ATT_NK)

    def fill(n):
        b, h = items[n]
        s_bufs[n % len(s_bufs)][0:window(b), :] = scores(b * tq, window(b), h)

    for n in range(min(ATT_AHEAD, len(items))):
        fill(n)
    for n, (b, h) in enumerate(items):
        if n + ATT_AHEAD < len(items):
            fill(n + ATT_AHEAD)
        softmax_pv(b * tq, window(b), h, s_bufs[n % len(s_bufs)][0:window(b), :])


def _attention(qt, k, vt, bias_t):
    bsz, d, s = qt.shape
    n_groups = d // LANES
    lane_spec = pl.BlockSpec((None, LANES, s), lambda g, b: (b, g, 0))
    return pl.pallas_call(
        _attn_kernel,
        grid=(n_groups, bsz),
        in_specs=[
            lane_spec,
            pl.BlockSpec((None, s, LANES), lambda g, b: (b, 0, g)),
            lane_spec,
            pl.BlockSpec((HEADS_PER_STEP,) + bias_t.shape[1:], lambda g, b: (g, 0, 0)),
        ],
        out_specs=lane_spec,
        out_shape=jax.ShapeDtypeStruct((bsz, d, s), BF16),
        scratch_shapes=[pltpu.VMEM((ATT_NK, ATT_TQ), F32)] * (ATT_AHEAD + 1),
        compiler_params=_params(("arbitrary", "arbitrary")),
        name="band_attn",
    )(qt, k, vt, bias_t)


def _attn_bias_table(rel_bias):
    n_heads = rel_bias.shape[0]
    n_m = (ATT_NK + ATT_TQ) // LANES - 1
    length = (n_m + 1) * LANES - 1
    dist = jnp.arange(length) - (LANES - 1) - (ATT_NK - LANES - ATT_HIST)
    per_dist = rel_bias[:, jnp.clip(dist, -MAX_REL_DIST, MAX_REL_DIST) + MAX_REL_DIST]
    ext = jnp.pad(per_dist.astype(F32), ((0, 0), (0, 1)))
    skew = jnp.tile(ext, (1, LANES))[:, :LANES * length].reshape(n_heads, LANES, length)
    table = skew[:, :, LANES - 1:]
    a = jnp.arange(LANES)[:, None]
    c = jnp.arange(n_m * LANES)[None, :]
    m, b = c // LANES, c % LANES
    chunk_gap = ((m - (ATT_NK // LANES - 1)) * (LANES // CHUNK) + LEFT_CHUNKS
                 + b // CHUNK - a // CHUNK)
    visible = (chunk_gap >= 0) & (chunk_gap <= LEFT_CHUNKS)
    return jnp.where(visible[None], table * LOG2_E, NEG_INF)


def _row(v):
    return v.reshape(1, -1)


def _cast_kernel(w_ref, o_ref):
    o_ref[...] = w_ref[...].astype(BF16)


def _layer_bf16(w, layer):
    _, r, c = w.shape
    tr = CAST_ROWS
    return pl.pallas_call(
        _cast_kernel,
        grid=(r // tr,),
        in_specs=[pl.BlockSpec((None, tr, c), lambda i: (layer, i, 0))],
        out_specs=pl.BlockSpec((tr, c), lambda i: (i, 0)),
        out_shape=jax.ShapeDtypeStruct((r, c), BF16),
        compiler_params=_params(("arbitrary",)),
        name="cast_bf16",
    )(w)


def kernel(x, p, mix_w_in, pool_w, pool_scale, conv_dw_w, conv_dw_b, conv_ln_g, conv_ln_b,
           mix_w_out, attn_w_qkv, attn_rel_bias, attn_w_o, ln_mix_g, ln_mix_b, ffn_w_up,
           ffn_dw_w, ffn_dw_b, ffn_w_down, ple_w_proj, ple_w_gate, ple_b_gate, ln_ffn_g,
           ln_ffn_b):
    def ffn(xi, i, attn_out=None):
        return _ffn(xi, p, i, _layer_bf16(ffn_w_up, i), ffn_dw_w[i], _row(ffn_dw_b[i]),
                    _layer_bf16(ffn_w_down, i), ple_w_proj[i].astype(BF16),
                    _layer_bf16(ple_w_gate, i), _row(ple_b_gate[i]),
                    _row(ln_ffn_g[i]), _row(ln_ffn_b[i]), attn_out=attn_out)

    x = _mixer(x, mix_w_in[0].astype(BF16), pool_w[0].astype(BF16), _row(pool_scale[0]),
               conv_dw_w[0], _row(conv_dw_b[0]), _row(conv_ln_g[0]), _row(conv_ln_b[0]),
               mix_w_out[0].astype(BF16), _row(ln_mix_g[0]), _row(ln_mix_b[0]))
    x = ffn(x, 0)

    w_qkv = attn_w_qkv[0]
    wqt = (w_qkv[:, :D_MODEL].T * (HEAD_DIM ** -0.5 * LOG2_E)).astype(BF16)
    wk = w_qkv[:, D_MODEL:2 * D_MODEL].astype(BF16)
    wvt = w_qkv[:, 2 * D_MODEL:].T.astype(BF16)
    qt, k, vt = _qkv(x, wqt, wk, wvt)
    heads_t = _attention(qt, k, vt, _attn_bias_table(attn_rel_bias[0]))
    x = ffn(x, 1, attn_out=(heads_t, attn_w_o[0].astype(BF16),
                            _row(ln_mix_g[1]), _row(ln_mix_b[1])))
    return x
```
